```python
import numpy as np
import jax
import jax.numpy as jnp
from jax import lax

D_MODEL = 2048
BATCH = 8
SEQ = 2048
DEPTH = 2

HEAD_DIM = D_MODEL // 16
H_MLSTM = 4
H_NA = 8
H_RET = 4
W_MLSTM = H_MLSTM * HEAD_DIM
W_NA = H_NA * HEAD_DIM
W_RET = H_RET * HEAD_DIM
D_MIX = W_MLSTM + W_NA + W_RET
IN_SIZES = (W_MLSTM, W_MLSTM, W_MLSTM, W_MLSTM, 4 * H_MLSTM, W_NA, W_NA, W_NA, W_RET, W_RET, W_RET, W_RET)
IN_COLS = sum(IN_SIZES)
D_FF = 11 * D_MODEL // 4
CONV_K = 3
CHUNK = 128
GRID_W = 64
WIN_R = 8
WIN_C = 16
QB_C = 16
KB_C = 32
ROPE_BASE = 10000.0
EPS = 1e-6
NEG = -1e30

kernel_name = 'hybrid_mlstm_natten_retention_macaron'


def rms_norm(x, g):
    xf = x.astype(jnp.float32)
    y = xf * lax.rsqrt(jnp.mean(xf * xf, axis=-1, keepdims=True) + EPS)
    return (y * g.astype(jnp.float32)).astype(x.dtype)


def head_norm(h, g, n_heads):
    B, T, _ = h.shape
    hh = h.reshape(B, T, n_heads, -1)
    hh = hh * lax.rsqrt(jnp.mean(hh * hh, axis=-1, keepdims=True) + EPS)
    return hh.reshape(B, T, -1) * g.astype(jnp.float32)


def swiglu(x, w_gu, w_down):
    gate, up = jnp.split(x @ w_gu, 2, axis=-1)
    return (jax.nn.silu(gate) * up) @ w_down


def to_heads(t, n_heads):
    B, T, _ = t.shape
    return t.reshape(B, T, n_heads, -1).transpose(0, 2, 1, 3)


def from_heads(t):
    B, H, T, d = t.shape
    return t.transpose(0, 2, 1, 3).reshape(B, T, H * d)


def to_chunks(t):
    B, H, T = t.shape[:3]
    return jnp.moveaxis(t.reshape(B, H, T // CHUNK, CHUNK, *t.shape[3:]), 2, 0)


def from_chunks(t):
    t = jnp.moveaxis(t, 0, 2)
    B, H, NC, L = t.shape[:4]
    return t.reshape(B, H, NC * L, *t.shape[4:])


def flip_t(t):
    return jnp.flip(t, axis=2)


def rope(x):
    T, d = x.shape[2], x.shape[3]
    inv = ROPE_BASE ** (-jnp.arange(0, d, 2, dtype=jnp.float32) / d)
    ang = jnp.arange(T, dtype=jnp.float32)[:, None] * inv[None, :]
    cos, sin = jnp.cos(ang), jnp.sin(ang)
    x1, x2 = jnp.split(x, 2, axis=-1)
    return jnp.concatenate([x1 * cos - x2 * sin, x1 * sin + x2 * cos], axis=-1)


def depthwise_conv(x, w, b):
    C = x.shape[-1]
    y = lax.conv_general_dilated(x, w.reshape(CONV_K, 1, C).astype(x.dtype), window_strides=(1,),
                                 padding=[(CONV_K // 2, CONV_K // 2)],
                                 dimension_numbers=('NWC', 'WIO', 'NWC'), feature_group_count=C)
    return y + b.astype(x.dtype)


def mlstm_chunk_scan(q, k, v, log_i, log_f):
    B, H, T, d = q.shape
    tri = jnp.tril(jnp.ones((CHUNK, CHUNK), dtype=bool))
    b_cum = jnp.cumsum(to_chunks(log_f), axis=-1)
    li = to_chunks(log_i)

    def step(carry, xs):
        C, n, m = carry
        qc, kc, vc, lic, bc = xs
        g = bc[..., -1]
        dmat = jnp.where(tri, bc[..., :, None] - bc[..., None, :] + lic[..., None, :], NEG)
        inter = bc + m[..., None]
        m_t = jnp.maximum(inter, jnp.max(dmat, axis=-1))
        s = jnp.einsum('bhtd,bhsd->bhts', qc, kc) * jnp.exp(dmat - m_t[..., None])
        w_inter = jnp.exp(inter - m_t)
        num = jnp.einsum('bhts,bhsd->bhtd', s, vc) + w_inter[..., None] * jnp.einsum('bhtd,bhde->bhte', qc, C)
        den = jnp.sum(s, axis=-1) + w_inter * jnp.einsum('bhtd,bhd->bht', qc, n)
        h = num / jnp.maximum(jnp.abs(den), jnp.exp(-m_t))[..., None]
        a = g[..., None] - bc + lic
        m_new = jnp.maximum(g + m, jnp.max(a, axis=-1))
        wk = jnp.exp(a - m_new[..., None])
        decay = jnp.exp(g + m - m_new)
        C = decay[..., None, None] * C + jnp.einsum('bhs,bhsd,bhse->bhde', wk, kc, vc)
        n = decay[..., None] * n + jnp.einsum('bhs,bhsd->bhd', wk, kc)
        return (C, n, m_new), h

    init = (jnp.zeros((B, H, d, d), jnp.float32), jnp.zeros((B, H, d), jnp.float32),
            jnp.full((B, H), NEG, jnp.float32))
    _, h = lax.scan(step, init, (to_chunks(q), to_chunks(k), to_chunks(v), li, b_cum))
    return from_chunks(h)


def retention_chunk_scan(q, k, v, log_gamma):
    B, H, T, d = q.shape
    pos = jnp.arange(CHUNK, dtype=jnp.float32)
    diff = pos[:, None] - pos[None, :]
    lower = diff >= 0
    intra = jnp.where(lower, jnp.exp(jnp.where(lower, diff, 0.0) * log_gamma[:, None, None]), 0.0)
    xi = jnp.exp((pos + 1.0) * log_gamma[:, None])
    zeta = jnp.exp((CHUNK - 1.0 - pos) * log_gamma[:, None])
    g_chunk = jnp.exp(CHUNK * log_gamma)

    def step(R, xs):
        qc, kc, vc = xs
        s = jnp.einsum('bhtd,bhsd->bhts', qc, kc) * intra
        o = jnp.einsum('bhts,bhsd->bhtd', s, vc) + xi[..., None] * jnp.einsum('bhtd,bhde->bhte', qc, R)
        R = g_chunk[:, None, None] * R + jnp.einsum('bhsd,bhse->bhde', kc * zeta[..., None], vc)
        return R, o

    _, o = lax.scan(step, jnp.zeros((B, H, d, d), jnp.float32), (to_chunks(q), to_chunks(k), to_chunks(v)))
    return from_chunks(o)


def neighborhood_attention(q, k, v, rpb, rows):
    B, H, T, d = q.shape
    wr = min(WIN_R, rows)
    n_cb = GRID_W // QB_C
    q_cols = np.arange(GRID_W).reshape(n_cb, QB_C)
    win_c0 = np.clip(q_cols - WIN_C // 2, 0, GRID_W - WIN_C)
    key_cols = np.clip(win_c0[:, 0], 0, GRID_W - KB_C)[:, None] + np.arange(KB_C)[None, :]
    kc = key_cols[:, None, :]
    col_ok = (kc >= win_c0[:, :, None]) & (kc < win_c0[:, :, None] + WIN_C)
    dc_idx = np.clip(kc - q_cols[:, :, None] + WIN_C - 1, 0, 2 * WIN_C - 2)
    col_bias = rpb.astype(jnp.float32)[:, :, dc_idx]
    col_mask = jnp.where(jnp.asarray(col_ok), 0.0, NEG).astype(jnp.float32)
    q5 = (q * d ** -0.5).reshape(B, H, rows, GRID_W, d)
    k5 = k.reshape(B, H, rows, GRID_W, d)
    v5 = v.reshape(B, H, rows, GRID_W, d)

    def row_block(r):
        r0 = jnp.clip(r - wr // 2, 0, rows - wr)
        k_blk = lax.dynamic_slice_in_dim(k5, r0, wr, axis=2)[:, :, :, key_cols]
        v_blk = lax.dynamic_slice_in_dim(v5, r0, wr, axis=2)[:, :, :, key_cols]
        q_blk = lax.dynamic_index_in_dim(q5, r, axis=2, keepdims=False).reshape(B, H, n_cb, QB_C, d)
        s = jnp.einsum('bhnqd,bhrnkd->bhnqrk', q_blk, k_blk).astype(jnp.float32)
        dr_idx = r0 + jnp.arange(wr) - r + WIN_R - 1
        bias = jnp.take(col_bias, dr_idx, axis=1).transpose(0, 2, 3, 1, 4)
        s = s + bias[None] + col_mask[None, None, :, :, None, :]
        p = jax.nn.softmax(s.reshape(B, H, n_cb, QB_C, wr * KB_C), axis=-1).reshape(s.shape)
        o = jnp.einsum('bhnqrk,bhrnkd->bhnqd', p.astype(v.dtype), v_blk)
        return o.reshape(B, H, GRID_W, d)

    out = lax.map(row_block, jnp.arange(rows))
    return jnp.moveaxis(out, 0, 2).reshape(B, H, T, d)


def token_mix(u, w_in, conv_w, conv_b, gate_b, m_gain, rpb, decay_logit, r_gain, w_out, rows):
    f32 = jnp.float32
    B, T, _ = u.shape
    proj = u @ w_in
    mq, mk, mv, mo, mg, nq, nk, nv, rq, rk, rv, rg = jnp.split(proj, np.cumsum(IN_SIZES)[:-1].tolist(), axis=-1)

    mqk = jax.nn.silu(depthwise_conv(jnp.concatenate([mq, mk], axis=-1), conv_w, conv_b))
    mq, mk = jnp.split(mqk, 2, axis=-1)
    q = to_heads(mq, H_MLSTM).astype(f32) * HEAD_DIM ** -0.5
    k = to_heads(mk, H_MLSTM).astype(f32)
    v = to_heads(mv, H_MLSTM).astype(f32)
    g = (mg.astype(f32) + gate_b.astype(f32).reshape(-1)).reshape(B, T, 4, H_MLSTM).transpose(2, 0, 3, 1)
    h_f = mlstm_chunk_scan(q, k, v, g[0], jax.nn.log_sigmoid(g[1]))
    h_b = flip_t(mlstm_chunk_scan(flip_t(q), flip_t(k), flip_t(v), flip_t(g[2]), flip_t(jax.nn.log_sigmoid(g[3]))))
    m_out = head_norm(jax.nn.sigmoid(mo.astype(f32)) * from_heads(h_f + h_b), m_gain, H_MLSTM)

    n_out = from_heads(neighborhood_attention(to_heads(nq, H_NA), to_heads(nk, H_NA), to_heads(nv, H_NA), rpb, rows))

    q = rope(to_heads(rq, H_RET).astype(f32))
    k = rope(to_heads(rk, H_RET).astype(f32)) * HEAD_DIM ** -0.5
    v = to_heads(rv, H_RET).astype(f32)
    lg = jax.nn.log_sigmoid(decay_logit.astype(f32))
    r = retention_chunk_scan(q, k, v, lg[0]) + flip_t(retention_chunk_scan(flip_t(q), flip_t(k), flip_t(v), lg[1]))
    r_out = jax.nn.silu(rg.astype(f32)) * head_norm(from_heads(r), r_gain, H_RET)

    mixed = jnp.concatenate([m_out.astype(u.dtype), n_out.astype(u.dtype), r_out.astype(u.dtype)], axis=-1)
    return mixed @ w_out


def setup_inputs(seed: int = 0) -> dict:
    key = jax.random.key(seed)
    ks = jax.random.split(key, 20)
    f32 = jnp.float32

    def normal(k, shape, scale):
        return jax.random.normal(k, shape, f32) * scale

    def gain(k, shape):
        return 1.0 + normal(k, shape, 0.02)

    gamma0 = 1.0 - 2.0 ** (-5.0 - np.arange(H_RET))
    logit0 = jnp.asarray(np.log(gamma0 / (1.0 - gamma0)), f32)
    f_bias = jnp.linspace(3.0, 6.0, H_MLSTM, dtype=f32)
    zeros_h = jnp.zeros((H_MLSTM,), f32)
    gate_off = jnp.stack([zeros_h, f_bias, zeros_h, f_bias])
    return {
        'x': normal(ks[0], (BATCH, SEQ, D_MODEL), 1.0),
        'ffn1_norm': gain(ks[1], (DEPTH, D_MODEL)),
        'ffn1_w_gu': normal(ks[2], (DEPTH, D_MODEL, 2 * D_FF), D_MODEL ** -0.5),
        'ffn1_w_down': normal(ks[3], (DEPTH, D_FF, D_MODEL), D_FF ** -0.5),
        'mix_norm': gain(ks[4], (DEPTH, D_MODEL)),
        'w_in': normal(ks[5], (DEPTH, D_MODEL, IN_COLS), D_MODEL ** -0.5),
        'mlstm_conv_w': normal(ks[6], (DEPTH, CONV_K, 2 * W_MLSTM), CONV_K ** -0.5),
        'mlstm_conv_b': normal(ks[7], (DEPTH, 2 * W_MLSTM), 0.02),
        'mlstm_gate_b': gate_off + normal(ks[8], (DEPTH, 4, H_MLSTM), 0.1),
        'mlstm_head_norm': gain(ks[9], (DEPTH, W_MLSTM)),
        'na_rpb': normal(ks[10], (DEPTH, H_NA, 2 * WIN_R - 1, 2 * WIN_C - 1), 0.1),
        'ret_decay_logit': logit0 + normal(ks[11], (DEPTH, 2, H_RET), 0.01),
        'ret_head_norm': gain(ks[12], (DEPTH, W_RET)),
        'w_out': normal(ks[13], (DEPTH, D_MIX, D_MODEL), D_MIX ** -0.5),
        'ffn2_norm': gain(ks[14], (DEPTH, D_MODEL)),
        'ffn2_w_gu': normal(ks[15], (DEPTH, D_MODEL, 2 * D_FF), D_MODEL ** -0.5),
        'ffn2_w_down': normal(ks[16], (DEPTH, D_FF, D_MODEL), D_FF ** -0.5),
        'final_norm': gain(ks[17], (D_MODEL,)),
    }


def reference(x, ffn1_norm, ffn1_w_gu, ffn1_w_down, mix_norm, w_in, mlstm_conv_w, mlstm_conv_b, mlstm_gate_b,
              mlstm_head_norm, na_rpb, ret_decay_logit, ret_head_norm, w_out, ffn2_norm, ffn2_w_gu, ffn2_w_down,
              final_norm):
    rows = x.shape[1] // GRID_W
    h = x
    for l in range(DEPTH):
        h = h + 0.5 * swiglu(rms_norm(h, ffn1_norm[l]), ffn1_w_gu[l], ffn1_w_down[l])
        h = h + token_mix(rms_norm(h, mix_norm[l]), w_in[l], mlstm_conv_w[l], mlstm_conv_b[l], mlstm_gate_b[l],
                          mlstm_head_norm[l], na_rpb[l], ret_decay_logit[l], ret_head_norm[l], w_out[l], rows)
        h = h + 0.5 * swiglu(rms_norm(h, ffn2_norm[l]), ffn2_w_gu[l], ffn2_w_down[l])
    return rms_norm(h, final_norm)
```

```python
import functools

import numpy as np
import jax
import jax.numpy as jnp
from jax import lax
from jax.experimental import pallas as pl
from jax.experimental.pallas import tpu as pltpu

F32 = jnp.float32
BF16 = jnp.bfloat16

HEAD_DIM = 128
H_MLSTM = 4
H_NA = 8
H_RET = 4
CHUNK = 128
GRID_W = 64
WIN_R = 8
WIN_C = 16
ROPE_BASE = 10000.0
EPS = 1e-6
NEG = -1e30

LANES = 128
VMEM_LIMIT_BYTES = 56 * 1024 * 1024

_BLK_MQ, _BLK_MK, _BLK_MV, _BLK_MO = 0, 4, 8, 12
_BLK_NQ, _BLK_NK, _BLK_NV = 16, 24, 32
_BLK_RQ, _BLK_RK, _BLK_RV, _BLK_RG = 40, 44, 48, 52
_N_MAIN_BLKS = 56


def _params(*sem):
    return pltpu.CompilerParams(dimension_semantics=sem, vmem_limit_bytes=VMEM_LIMIT_BYTES)


def _sigmoid(x):
    return 1.0 / (1.0 + jnp.exp(-x))


def _log_sigmoid(x):
    return jnp.minimum(x, 0.0) - jnp.log1p(jnp.exp(-jnp.abs(x)))


def _rms(x, g):
    return x * lax.rsqrt(jnp.mean(x * x, axis=-1, keepdims=True) + EPS) * g


def _dot(a, b):
    return jnp.dot(a, b, preferred_element_type=F32)


def _dot_nt(a, b):
    return lax.dot_general(a, b, (((1,), (1,)), ((), ())), preferred_element_type=F32)


def _dot_tn(a, b):
    return lax.dot_general(a, b, (((0,), (0,)), ((), ())), preferred_element_type=F32)


def _ffn_kernel(x_ref, g_ref, wg_ref, wu_ref, wd_ref, fg_ref, o_ref, xn_ref, *, final_norm):
    j = pl.program_id(1)

    @pl.when(j == 0)
    def _():
        xn_ref[...] = _rms(x_ref[...], g_ref[...]).astype(BF16)
        o_ref[...] = jnp.zeros_like(o_ref)

    xn = xn_ref[...]
    gate = _dot(xn, wg_ref[...])
    up = _dot(xn, wu_ref[...])
    act = (gate * _sigmoid(gate) * up).astype(BF16)
    o_ref[...] += _dot(act, wd_ref[...])

    @pl.when(j == pl.num_programs(1) - 1)
    def _():
        h = x_ref[...] + 0.5 * o_ref[...]
        if final_norm:
            h = _rms(h, fg_ref[...])
        o_ref[...] = h


def _ffn(h, g, w_gu, w_down, final_g, *, final_norm, tm, tf):
    n, d = h.shape
    f = w_down.shape[0]
    nf = f // tf
    return pl.pallas_call(
        functools.partial(_ffn_kernel, final_norm=final_norm),
        grid=(n // tm, nf),
        in_specs=[
            pl.BlockSpec((tm, d), lambda i, j: (i, 0), pipeline_mode=pl.Buffered(1)),
            pl.BlockSpec((1, d), lambda i, j: (0, 0)),
            pl.BlockSpec((d, tf), lambda i, j: (0, j)),
            pl.BlockSpec((d, tf), lambda i, j: (0, j + nf)),
            pl.BlockSpec((tf, d), lambda i, j: (j, 0)),
            pl.BlockSpec((1, d), lambda i, j: (0, 0)),
        ],
        out_specs=pl.BlockSpec((tm, d), lambda i, j: (i, 0)),
        out_shape=jax.ShapeDtypeStruct((n, d), F32),
        scratch_shapes=[pltpu.VMEM((tm, d), BF16)],
        compiler_params=_params("parallel", "arbitrary"),
        name="ffn",
    )(h, g.reshape(1, d), w_gu, w_gu, w_down, final_g.reshape(1, d))


def _mix_in_kernel(x_ref, g_ref, w_ref, wgate_ref, o_ref, og_ref, xn_ref):
    @pl.when(pl.program_id(1) == 0)
    def _():
        xn = _rms(x_ref[...], g_ref[...]).astype(BF16)
        xn_ref[...] = xn
        og_ref[...] = _dot(xn, wgate_ref[...])

    o_ref[...] = _dot(xn_ref[...], w_ref[...])


def _mix_in(h, g, w_main, w_gate, *, tm, tn):
    n, d = h.shape
    c = w_main.shape[1]
    return pl.pallas_call(
        _mix_in_kernel,
        grid=(n // tm, c // tn),
        in_specs=[
            pl.BlockSpec((tm, d), lambda i, j: (i, 0)),
            pl.BlockSpec((1, d), lambda i, j: (0, 0)),
            pl.BlockSpec((d, tn), lambda i, j: (0, j)),
            pl.BlockSpec((d, LANES), lambda i, j: (0, 0)),
        ],
        out_specs=[
            pl.BlockSpec((tm, tn), lambda i, j: (i, j)),
            pl.BlockSpec((tm, LANES), lambda i, j: (i, 0)),
        ],
        out_shape=[jax.ShapeDtypeStruct((n, c), F32), jax.ShapeDtypeStruct((n, LANES), F32)],
        scratch_shapes=[pltpu.VMEM((tm, d), BF16)],
        compiler_params=_params("parallel", "arbitrary"),
        name="mix_in",
    )(h, g.reshape(1, d), w_main, w_gate)


def _mix_out_kernel(h_ref, m_ref, n_ref, r_ref, wm_ref, wn_ref, wr_ref, o_ref):
    acc = _dot(m_ref[...], wm_ref[...]) + _dot(n_ref[...], wn_ref[...]) + _dot(r_ref[...], wr_ref[...])
    o_ref[...] = h_ref[...] + acc


def _mix_out(h, m_out, n_out, r_out, w_m, w_n, w_r, *, tm, tn):
    n, d = h.shape
    return pl.pallas_call(
        _mix_out_kernel,
        grid=(n // tm, d // tn),
        in_specs=[
            pl.BlockSpec((tm, tn), lambda i, j: (i, j)),
            pl.BlockSpec((tm, m_out.shape[1]), lambda i, j: (i, 0)),
            pl.BlockSpec((tm, n_out.shape[1]), lambda i, j: (i, 0)),
            pl.BlockSpec((tm, r_out.shape[1]), lambda i, j: (i, 0)),
            pl.BlockSpec((w_m.shape[0], tn), lambda i, j: (0, j)),
            pl.BlockSpec((w_n.shape[0], tn), lambda i, j: (0, j)),
            pl.BlockSpec((w_r.shape[0], tn), lambda i, j: (0, j)),
        ],
        out_specs=pl.BlockSpec((tm, tn), lambda i, j: (i, j)),
        out_shape=jax.ShapeDtypeStruct((n, d), F32),
        compiler_params=_params("parallel", "arbitrary"),
        name="mix_out",
    )(h, m_out, n_out, r_out, w_m, w_n, w_r)


def _mlstm_step(qc, kc, vc, b_col, u_col, u_row, mask, g, carry):
    c_st, n_st, m_st = carry
    qb, kb, vb = qc.astype(BF16), kc.astype(BF16), vc.astype(BF16)
    dmat = jnp.where(mask, b_col + u_row, NEG)
    inter = b_col + m_st
    m_t = jnp.maximum(inter, jnp.max(dmat, axis=-1, keepdims=True))
    s = _dot_nt(qb, kb) * jnp.exp(dmat - m_t)
    w_inter = jnp.exp(inter - m_t)
    num = _dot(s.astype(BF16), vb) + w_inter * _dot(qb, c_st.astype(BF16))
    den = jnp.sum(s, axis=-1, keepdims=True) + w_inter * jnp.sum(qc * n_st, axis=-1, keepdims=True)
    h_out = num / jnp.maximum(jnp.abs(den), jnp.exp(-m_t))
    a = g + u_col
    m_new = jnp.maximum(g + m_st, jnp.max(a, axis=0, keepdims=True))
    kw = kc * jnp.exp(a - m_new)
    decay = jnp.exp(g + m_st - m_new)
    c_new = decay * c_st + _dot_tn(kw.astype(BF16), vb)
    n_new = decay * n_st + jnp.sum(kw, axis=0, keepdims=True)
    return h_out, (c_new, n_new, m_new)


def _mlstm_kernel(q_ref, k_ref, v_ref, o_ref, gt_ref, cwq_ref, cwk_ref, cbq_ref, cbk_ref, gb_ref, gain_ref,
                  out_ref, qs_ref, ks_ref, bp_ref, bs_ref, u_ref, hf_ref, hb_ref):
    t_len = q_ref.shape[0]
    n_chunks = t_len // CHUNK
    head = pl.program_id(1)
    row = lax.broadcasted_iota(jnp.int32, (t_len, 1), 0)

    def conv_silu(x, w_ref, b_ref):
        prev = jnp.where(row >= 1, pltpu.roll(x, 1, 0), 0.0)
        nxt = jnp.where(row <= t_len - 2, pltpu.roll(x, t_len - 1, 0), 0.0)
        y = prev * w_ref[0:1, :] + x * w_ref[1:2, :] + nxt * w_ref[2:3, :] + b_ref[...]
        return y * _sigmoid(y)

    qs_ref[...] = conv_silu(q_ref[...], cwq_ref, cbq_ref) * (HEAD_DIM ** -0.5)
    ks_ref[...] = conv_silu(k_ref[...], cwk_ref, cbk_ref)

    gates = gt_ref[...] + gb_ref[...]
    logf = _log_sigmoid(gates)
    rin = row & (CHUNK - 1)
    bp = logf
    bs = logf
    k = 1
    while k < CHUNK:
        bp = bp + jnp.where(rin >= k, pltpu.roll(bp, k, 0), 0.0)
        bs = bs + jnp.where(rin < CHUNK - k, pltpu.roll(bs, t_len - k, 0), 0.0)
        k *= 2
    bp_ref[...] = bp
    bs_ref[...] = bs
    lane = lax.broadcasted_iota(jnp.int32, (1, LANES), 1)
    u_ref[...] = gates - jnp.where(lane < 2 * H_MLSTM, pltpu.roll(bp, LANES - H_MLSTM, 1),
                                   pltpu.roll(bs, LANES - H_MLSTM, 1))

    ti = lax.broadcasted_iota(jnp.int32, (CHUNK, CHUNK), 0)
    si = lax.broadcasted_iota(jnp.int32, (CHUNK, CHUNK), 1)
    mask_f = ti >= si
    mask_b = ti <= si
    sub = lax.broadcasted_iota(jnp.int32, (LANES, 1), 0)

    def col(x, idx):
        return jnp.sum(jnp.where(lane == idx, x, 0.0), axis=-1, keepdims=True)

    def rowvec(x, idx):
        return jnp.sum(jnp.where(sub == idx, x.T, 0.0), axis=0, keepdims=True)

    def body(i, carry):
        cf, cb = carry
        sf = pl.ds(pl.multiple_of(i * CHUNK, CHUNK), CHUNK)
        sb = pl.ds(pl.multiple_of((n_chunks - 1 - i) * CHUNK, CHUNK), CHUNK)

        uf = u_ref[sf, :]
        bf_col = col(bp_ref[sf, :], H_MLSTM + head)
        hf, cf = _mlstm_step(qs_ref[sf, :], ks_ref[sf, :], v_ref[sf, :], bf_col, col(uf, head),
                             rowvec(uf, head), mask_f, bf_col[CHUNK - 1:CHUNK, :], cf)
        hf_ref[sf, :] = hf

        ub = u_ref[sb, :]
        bb_col = col(bs_ref[sb, :], 3 * H_MLSTM + head)
        hb, cb = _mlstm_step(qs_ref[sb, :], ks_ref[sb, :], v_ref[sb, :], bb_col, col(ub, 2 * H_MLSTM + head),
                             rowvec(ub, 2 * H_MLSTM + head), mask_b, bb_col[0:1, :], cb)
        hb_ref[sb, :] = hb
        return cf, cb

    init = (jnp.zeros((HEAD_DIM, HEAD_DIM), F32), jnp.zeros((1, HEAD_DIM), F32), jnp.full((1, 1), NEG, F32))
    lax.fori_loop(0, n_chunks, body, (init, init))

    y = _sigmoid(o_ref[...]) * (hf_ref[...] + hb_ref[...])
    out_ref[...] = _rms(y, gain_ref[...]).astype(out_ref.dtype)


def _mlstm(proj, gates, conv_w, conv_b, gate_b, gain, *, batch, t_len):
    w = H_MLSTM * HEAD_DIM
    blk = lambda off: pl.BlockSpec((t_len, HEAD_DIM), lambda b, h: (b, off + h))
    vec = lambda off: pl.BlockSpec((1, HEAD_DIM), lambda b, h: (0, off + h))
    seq = pltpu.VMEM((t_len, HEAD_DIM), F32)
    return pl.pallas_call(
        _mlstm_kernel,
        grid=(batch, H_MLSTM),
        in_specs=[
            blk(_BLK_MQ), blk(_BLK_MK), blk(_BLK_MV), blk(_BLK_MO),
            pl.BlockSpec((t_len, LANES), lambda b, h: (b, 0)),
            pl.BlockSpec((3, HEAD_DIM), lambda b, h: (0, h)),
            pl.BlockSpec((3, HEAD_DIM), lambda b, h: (0, H_MLSTM + h)),
            vec(0), vec(H_MLSTM),
            pl.BlockSpec((1, LANES), lambda b, h: (0, 0)),
            vec(0),
        ],
        out_specs=pl.BlockSpec((t_len, HEAD_DIM), lambda b, h: (b, h)),
        out_shape=jax.ShapeDtypeStruct((batch * t_len, w), BF16),
        scratch_shapes=[seq] * 7,
        compiler_params=_params("parallel", "parallel"),
        name="mlstm",
    )(proj, proj, proj, proj, gates, conv_w, conv_w, conv_b.reshape(1, -1), conv_b.reshape(1, -1),
      gate_b, gain.reshape(1, -1))


def _na_kernel(q_ref, k_ref, v_ref, bias_ref, out_ref):
    rows = q_ref.shape[0] // GRID_W
    wr = min(WIN_R, rows)
    n_keys = wr * GRID_W

    def body(r, carry):
        r0 = jnp.clip(r - wr // 2, 0, rows - wr)
        sq = pl.ds(pl.multiple_of(r * GRID_W, GRID_W), GRID_W)
        sk = pl.ds(pl.multiple_of(r0 * GRID_W, GRID_W), n_keys)
        qr = (q_ref[sq, :] * (HEAD_DIM ** -0.5)).astype(BF16)
        s = _dot_nt(qr, k_ref[sk, :].astype(BF16)) + bias_ref[0, r - r0]
        e = jnp.exp(s - jnp.max(s, axis=-1, keepdims=True))
        o = _dot(e.astype(BF16), v_ref[sk, :].astype(BF16)) / jnp.sum(e, axis=-1, keepdims=True)
        out_ref[sq, :] = o.astype(out_ref.dtype)
        return carry

    lax.fori_loop(0, rows, body, 0)


def _na_bias_table(rpb, rows):
    wr = min(WIN_R, rows)
    off = np.arange(wr)[:, None, None, None]
    qc = np.arange(GRID_W)[None, :, None, None]
    kr = np.arange(wr)[None, None, :, None]
    kc = np.arange(GRID_W)[None, None, None, :]
    win_c0 = np.clip(qc - WIN_C // 2, 0, GRID_W - WIN_C)
    ok = np.broadcast_to((kc >= win_c0) & (kc < win_c0 + WIN_C), (wr, GRID_W, wr, GRID_W))
    dr = np.broadcast_to(np.clip(kr - off + WIN_R - 1, 0, 2 * WIN_R - 2), ok.shape)
    dc = np.broadcast_to(np.clip(kc - qc + WIN_C - 1, 0, 2 * WIN_C - 2), ok.shape)
    tab = jnp.where(jnp.asarray(ok), rpb.astype(F32)[:, dr, dc], NEG)
    return tab.reshape(rpb.shape[0], wr, GRID_W, wr * GRID_W)


def _na(proj, bias_tab, *, batch, t_len):
    blk = lambda off: pl.BlockSpec((t_len, HEAD_DIM), lambda b, h: (b, off + h))
    return pl.pallas_call(
        _na_kernel,
        grid=(batch, H_NA),
        in_specs=[
            blk(_BLK_NQ), blk(_BLK_NK), blk(_BLK_NV),
            pl.BlockSpec((1,) + bias_tab.shape[1:], lambda b, h: (h, 0, 0, 0)),
        ],
        out_specs=pl.BlockSpec((t_len, HEAD_DIM), lambda b, h: (b, h)),
        out_shape=jax.ShapeDtypeStruct((batch * t_len, H_NA * HEAD_DIM), BF16),
        compiler_params=_params("parallel", "parallel"),
        name="natten",
    )(proj, proj, proj, bias_tab)


def _ret_kernel(q_ref, k_ref, v_ref, g_ref, cos_ref, sin_ref, dl_ref, gain_ref, out_ref,
                qs_ref, ks_ref, of_ref, ob_ref):
    t_len = q_ref.shape[0]
    n_chunks = t_len // CHUNK

    def rope(x):
        return x * cos_ref[...] + pltpu.roll(x, HEAD_DIM // 2, 1) * sin_ref[...]

    qs_ref[...] = rope(q_ref[...])
    ks_ref[...] = rope(k_ref[...]) * (HEAD_DIM ** -0.5)

    lg = _log_sigmoid(dl_ref[0])
    lg_f, lg_b = lg[0:1, :], lg[1:2, :]
    diff = (lax.broadcasted_iota(jnp.int32, (CHUNK, CHUNK), 0)
            - lax.broadcasted_iota(jnp.int32, (CHUNK, CHUNK), 1)).astype(F32)
    intra_f = jnp.where(diff >= 0, jnp.exp(jnp.where(diff >= 0, diff, 0.0) * lg_f), 0.0)
    intra_b = jnp.where(diff <= 0, jnp.exp(jnp.where(diff <= 0, -diff, 0.0) * lg_b), 0.0)
    pos = lax.broadcasted_iota(jnp.int32, (CHUNK, HEAD_DIM), 0).astype(F32)
    xi_f = jnp.exp((pos + 1.0) * lg_f)
    zeta_f = jnp.exp((CHUNK - 1.0 - pos) * lg_f)
    xi_b = jnp.exp((CHUNK - pos) * lg_b)
    zeta_b = jnp.exp(pos * lg_b)
    gch_f = jnp.exp(CHUNK * lg_f)
    gch_b = jnp.exp(CHUNK * lg_b)

    def step(sl, intra, xi, zeta, gch, r_st):
        qc, kc = qs_ref[sl, :], ks_ref[sl, :]
        qb, vb = qc.astype(BF16), v_ref[sl, :].astype(BF16)
        s = _dot_nt(qb, kc.astype(BF16)) * intra
        o = _dot(s.astype(BF16), vb) + xi * _dot(qb, r_st.astype(BF16))
        r_new = gch * r_st + _dot_tn((kc * zeta).astype(BF16), vb)
        return o, r_new

    def body(i, carry):
        rf, rb = carry
        sf = pl.ds(pl.multiple_of(i * CHUNK, CHUNK), CHUNK)
        sb = pl.ds(pl.multiple_of((n_chunks - 1 - i) * CHUNK, CHUNK), CHUNK)
        o, rf = step(sf, intra_f, xi_f, zeta_f, gch_f, rf)
        of_ref[sf, :] = o
        o, rb = step(sb, intra_b, xi_b, zeta_b, gch_b, rb)
        ob_ref[sb, :] = o
        return rf, rb

    zero = jnp.zeros((HEAD_DIM, HEAD_DIM), F32)
    lax.fori_loop(0, n_chunks, body, (zero, zero))

    g = g_ref[...]
    out_ref[...] = (g * _sigmoid(g) * _rms(of_ref[...] + ob_ref[...], gain_ref[...])).astype(out_ref.dtype)


def _ret(proj, cos2, sin2, decay_rows, gain, *, batch, t_len):
    blk = lambda off: pl.BlockSpec((t_len, HEAD_DIM), lambda b, h: (b, off + h))
    tab = pl.BlockSpec((t_len, HEAD_DIM), lambda b, h: (0, 0))
    seq = pltpu.VMEM((t_len, HEAD_DIM), F32)
    return pl.pallas_call(
        _ret_kernel,
        grid=(batch, H_RET),
        in_specs=[
            blk(_BLK_RQ), blk(_BLK_RK), blk(_BLK_RV), blk(_BLK_RG), tab, tab,
            pl.BlockSpec((1, 2, LANES), lambda b, h: (h, 0, 0)),
            pl.BlockSpec((1, HEAD_DIM), lambda b, h: (0, h)),
        ],
        out_specs=pl.BlockSpec((t_len, HEAD_DIM), lambda b, h: (b, h)),
        out_shape=jax.ShapeDtypeStruct((batch * t_len, H_RET * HEAD_DIM), BF16),
        scratch_shapes=[seq] * 4,
        compiler_params=_params("parallel", "parallel"),
        name="retention",
    )(proj, proj, proj, proj, cos2, sin2, decay_rows, gain.reshape(1, -1))


def _rope_tables(t_len):
    inv = ROPE_BASE ** (-jnp.arange(0, HEAD_DIM, 2, dtype=F32) / HEAD_DIM)
    ang = jnp.arange(t_len, dtype=F32)[:, None] * inv[None, :]
    cos, sin = jnp.cos(ang), jnp.sin(ang)
    return jnp.concatenate([cos, cos], axis=-1), jnp.concatenate([-sin, sin], axis=-1)


def _pick(n, pref):
    return pref if n % pref == 0 else n


def kernel(x, ffn1_norm, ffn1_w_gu, ffn1_w_down, mix_norm, w_in, mlstm_conv_w, mlstm_conv_b, mlstm_gate_b,
           mlstm_head_norm, na_rpb, ret_decay_logit, ret_head_norm, w_out, ffn2_norm, ffn2_w_gu, ffn2_w_down,
           final_norm):
    batch, t_len, d = x.shape
    n = batch * t_len
    depth = ffn1_norm.shape[0]
    rows = t_len // GRID_W
    w_m, w_n, w_r = H_MLSTM * HEAD_DIM, H_NA * HEAD_DIM, H_RET * HEAD_DIM
    n_gates = 4 * H_MLSTM
    gate_lo = 4 * w_m

    tm = _pick(n, 1024)
    ffn = functools.partial(_ffn, tm=tm, tf=512)
    cos2, sin2 = _rope_tables(t_len)

    h = x.reshape(n, d)
    for l in range(depth):
        h = ffn(h, ffn1_norm[l], ffn1_w_gu[l].astype(BF16), ffn1_w_down[l].astype(BF16), final_norm,
                final_norm=False)

        w_main = jnp.concatenate([w_in[l][:, :gate_lo], w_in[l][:, gate_lo + n_gates:]], axis=1).astype(BF16)
        w_gate = jnp.pad(w_in[l][:, gate_lo:gate_lo + n_gates], ((0, 0), (0, LANES - n_gates))).astype(BF16)
        proj, gates = _mix_in(h, mix_norm[l], w_main, w_gate, tm=tm, tn=1024)

        gate_b = jnp.pad(mlstm_gate_b[l].astype(F32).reshape(1, n_gates), ((0, 0), (0, LANES - n_gates)))
        m_out = _mlstm(proj, gates, mlstm_conv_w[l], mlstm_conv_b[l], gate_b, mlstm_head_norm[l],
                       batch=batch, t_len=t_len)
        n_out = _na(proj, _na_bias_table(na_rpb[l], rows), batch=batch, t_len=t_len)
        decay_rows = jnp.broadcast_to(ret_decay_logit[l].astype(F32).T[:, :, None], (H_RET, 2, LANES))
        r_out = _ret(proj, cos2, sin2, decay_rows, ret_head_norm[l], batch=batch, t_len=t_len)

        wo = w_out[l].astype(BF16)
        h = _mix_out(h, m_out, n_out, r_out, wo[:w_m], wo[w_m:w_m + w_n], wo[w_m + w_n:], tm=tm, tn=1024)

        h = ffn(h, ffn2_norm[l], ffn2_w_gu[l].astype(BF16), ffn2_w_down[l].astype(BF16), final_norm,
                final_norm=(l == depth - 1))
    return h.reshape(batch, t_len, d)
```

```python
import functools

import numpy as np
import jax
import jax.numpy as jnp
from jax import lax
from jax.experimental import pallas as pl
from jax.experimental.pallas import tpu as pltpu

F32 = jnp.float32
BF16 = jnp.bfloat16

HEAD_DIM = 128
H_MLSTM = 4
H_NA = 8
H_RET = 4
CHUNK = 128
GRID_W = 64
WIN_R = 8
WIN_C = 16
ROPE_BASE = 10000.0
EPS = 1e-6
NEG = -1e30

LANES = 128
VMEM_LIMIT_BYTES = 56 * 1024 * 1024

_BLK_MQ, _BLK_MK, _BLK_MV, _BLK_MO = 0, 4, 8, 12
_BLK_NQ, _BLK_NK, _BLK_NV = 16, 24, 32
_BLK_RQ, _BLK_RK, _BLK_RV, _BLK_RG = 40, 44, 48, 52
_N_MAIN_BLKS = 56


def _params(*sem):
    return pltpu.CompilerParams(dimension_semantics=sem, vmem_limit_bytes=VMEM_LIMIT_BYTES)


def _sigmoid(x):
    return 1.0 / (1.0 + jnp.exp(-x))


def _log_sigmoid(x):
    return jnp.minimum(x, 0.0) - jnp.log1p(jnp.exp(-jnp.abs(x)))


def _rms(x, g):
    return x * lax.rsqrt(jnp.mean(x * x, axis=-1, keepdims=True) + EPS) * g


def _dot(a, b):
    return jnp.dot(a, b, preferred_element_type=F32)


def _dot_nt(a, b):
    return lax.dot_general(a, b, (((1,), (1,)), ((), ())), preferred_element_type=F32)


def _dot_tn(a, b):
    return lax.dot_general(a, b, (((0,), (0,)), ((), ())), preferred_element_type=F32)


def _ffn_kernel(x_ref, g_ref, wg_ref, wu_ref, wd_ref, fg_ref, o_ref, xn_ref, *, final_norm):
    j = pl.program_id(1)

    @pl.when(j == 0)
    def _():
        xn_ref[...] = _rms(x_ref[...], g_ref[...]).astype(BF16)
        o_ref[...] = jnp.zeros_like(o_ref)

    xn = xn_ref[...]
    gate = _dot(xn, wg_ref[...])
    up = _dot(xn, wu_ref[...])
    act = (gate * _sigmoid(gate) * up).astype(BF16)
    o_ref[...] += _dot(act, wd_ref[...])

    @pl.when(j == pl.num_programs(1) - 1)
    def _():
        h = x_ref[...] + 0.5 * o_ref[...]
        if final_norm:
            h = _rms(h, fg_ref[...])
        o_ref[...] = h


def _ffn(h, g, w_gu, w_down, final_g, *, final_norm, tm, tf):
    n, d = h.shape
    f = w_down.shape[0]
    nf = f // tf
    return pl.pallas_call(
        functools.partial(_ffn_kernel, final_norm=final_norm),
        grid=(n // tm, nf),
        in_specs=[
            pl.BlockSpec((tm, d), lambda i, j: (i, 0), pipeline_mode=pl.Buffered(1)),
            pl.BlockSpec((1, d), lambda i, j: (0, 0)),
            pl.BlockSpec((d, tf), lambda i, j: (0, j)),
            pl.BlockSpec((d, tf), lambda i, j: (0, j + nf)),
            pl.BlockSpec((tf, d), lambda i, j: (j, 0)),
            pl.BlockSpec((1, d), lambda i, j: (0, 0)),
        ],
        out_specs=pl.BlockSpec((tm, d), lambda i, j: (i, 0)),
        out_shape=jax.ShapeDtypeStruct((n, d), F32),
        scratch_shapes=[pltpu.VMEM((tm, d), BF16)],
        compiler_params=_params("parallel", "arbitrary"),
        name="ffn",
    )(h, g.reshape(1, d), w_gu, w_gu, w_down, final_g.reshape(1, d))


def _mix_in_kernel(x_ref, g_ref, w_ref, wgate_ref, o_ref, og_ref, xn_ref):
    @pl.when(pl.program_id(1) == 0)
    def _():
        xn = _rms(x_ref[...], g_ref[...]).astype(BF16)
        xn_ref[...] = xn
        og_ref[...] = _dot(xn, wgate_ref[...])

    o_ref[...] = _dot(xn_ref[...], w_ref[...])


def _mix_in(h, g, w_main, w_gate, *, tm, tn):
    n, d = h.shape
    c = w_main.shape[1]
    return pl.pallas_call(
        _mix_in_kernel,
        grid=(n // tm, c // tn),
        in_specs=[
            pl.BlockSpec((tm, d), lambda i, j: (i, 0)),
            pl.BlockSpec((1, d), lambda i, j: (0, 0)),
            pl.BlockSpec((d, tn), lambda i, j: (0, j)),
            pl.BlockSpec((d, LANES), lambda i, j: (0, 0)),
        ],
        out_specs=[
            pl.BlockSpec((tm, tn), lambda i, j: (i, j)),
            pl.BlockSpec((tm, LANES), lambda i, j: (i, 0)),
        ],
        out_shape=[jax.ShapeDtypeStruct((n, c), F32), jax.ShapeDtypeStruct((n, LANES), F32)],
        scratch_shapes=[pltpu.VMEM((tm, d), BF16)],
        compiler_params=_params("parallel", "arbitrary"),
        name="mix_in",
    )(h, g.reshape(1, d), w_main, w_gate)


def _mix_out_kernel(h_ref, m_ref, n_ref, r_ref, wm_ref, wn_ref, wr_ref, o_ref):
    acc = _dot(m_ref[...], wm_ref[...]) + _dot(n_ref[...], wn_ref[...]) + _dot(r_ref[...], wr_ref[...])
    o_ref[...] = h_ref[...] + acc


def _mix_out(h, m_out, n_out, r_out, w_m, w_n, w_r, *, tm, tn):
    n, d = h.shape
    return pl.pallas_call(
        _mix_out_kernel,
        grid=(n // tm, d // tn),
        in_specs=[
            pl.BlockSpec((tm, tn), lambda i, j: (i, j)),
            pl.BlockSpec((tm, m_out.shape[1]), lambda i, j: (i, 0)),
            pl.BlockSpec((tm, n_out.shape[1]), lambda i, j: (i, 0)),
            pl.BlockSpec((tm, r_out.shape[1]), lambda i, j: (i, 0)),
            pl.BlockSpec((w_m.shape[0], tn), lambda i, j: (0, j)),
            pl.BlockSpec((w_n.shape[0], tn), lambda i, j: (0, j)),
            pl.BlockSpec((w_r.shape[0], tn), lambda i, j: (0, j)),
        ],
        out_specs=pl.BlockSpec((tm, tn), lambda i, j: (i, j)),
        out_shape=jax.ShapeDtypeStruct((n, d), F32),
        compiler_params=_params("parallel", "arbitrary"),
        name="mix_out",
    )(h, m_out, n_out, r_out, w_m, w_n, w_r)


def _mlstm_step(qb, kb, kt, va, b, u_row, mask, g, st_ref, m_st):
    dmat = jnp.where(mask, b + u_row, NEG)
    inter = b + m_st
    m_t = jnp.maximum(inter, jnp.max(dmat, axis=-1, keepdims=True))
    s = _dot_nt(qb, kb) * jnp.exp(dmat - m_t)
    st = st_ref[...]
    r_intra = _dot(s.astype(BF16), va)
    r_inter = _dot(qb, st.astype(BF16))
    w_inter = jnp.exp(inter - m_t)
    num = r_intra[:, :HEAD_DIM] + w_inter * r_inter[:, :HEAD_DIM]
    den = r_intra[:, HEAD_DIM:] + w_inter * r_inter[:, HEAD_DIM:]
    h_out = num / jnp.maximum(jnp.abs(den), jnp.exp(-m_t))
    a = g + u_row
    m_new = jnp.maximum(g + m_st, jnp.max(a, axis=-1, keepdims=True))
    ktw = (kt * jnp.exp(a - m_new)).astype(BF16)
    st_ref[...] = jnp.exp(g + m_st - m_new) * st + _dot(ktw, va)
    return h_out, m_new


def _mlstm_kernel(q_ref, k_ref, v_ref, o_ref, gt_ref, cwq_ref, cwk_ref, cbq_ref, cbk_ref, gb_ref, gain_ref,
                  out_ref, qs_ref, ks_ref, kt_ref, va_ref, bp_ref, bs_ref, ut_ref, bf_ref, bb_ref, hf_ref, hb_ref,
                  sf_ref, sb_ref):
    t_len = q_ref.shape[0]
    n_chunks = t_len // CHUNK
    head = pl.program_id(1)
    row = lax.broadcasted_iota(jnp.int32, (t_len, 1), 0)
    lane = lax.broadcasted_iota(jnp.int32, (1, LANES), 1)

    @pl.when(head == 0)
    def _():
        gates = gt_ref[...] + gb_ref[...]
        logf = _log_sigmoid(gates)
        rin = row & (CHUNK - 1)
        bp = logf
        bs = logf
        k = 1
        while k < CHUNK:
            bp = bp + jnp.where(rin >= k, pltpu.roll(bp, k, 0), 0.0)
            bs = bs + jnp.where(rin < CHUNK - k, pltpu.roll(bs, t_len - k, 0), 0.0)
            k *= 2
        bp_ref[...] = bp
        bs_ref[...] = bs
        u = gates - jnp.where(lane < 2 * H_MLSTM, pltpu.roll(bp, LANES - H_MLSTM, 1),
                              pltpu.roll(bs, LANES - H_MLSTM, 1))
        for c in range(n_chunks):
            ut_ref[:, c * CHUNK:(c + 1) * CHUNK] = u[c * CHUNK:(c + 1) * CHUNK, :].T[0:ut_ref.shape[0], :]

    def conv_silu(x, w_ref, b_ref):
        prev = jnp.where(row >= 1, pltpu.roll(x, 1, 0), 0.0)
        nxt = jnp.where(row <= t_len - 2, pltpu.roll(x, t_len - 1, 0), 0.0)
        y = prev * w_ref[0:1, :] + x * w_ref[1:2, :] + nxt * w_ref[2:3, :] + b_ref[...]
        return y * _sigmoid(y)

    def lane_bcast(x, idx):
        return jnp.broadcast_to(jnp.sum(jnp.where(lane == idx, x, 0.0), axis=-1, keepdims=True), x.shape)

    qs_ref[...] = (conv_silu(q_ref[...], cwq_ref, cbq_ref) * (HEAD_DIM ** -0.5)).astype(BF16)
    k = conv_silu(k_ref[...], cwk_ref, cbk_ref)
    ks_ref[...] = k.astype(BF16)
    for c in range(n_chunks):
        kt_ref[:, c * CHUNK:(c + 1) * CHUNK] = k[c * CHUNK:(c + 1) * CHUNK, :].T
    va_ref[:, :HEAD_DIM] = v_ref[...].astype(BF16)
    va_ref[:, HEAD_DIM:] = jnp.ones((t_len, HEAD_DIM), BF16)
    bf_ref[...] = lane_bcast(bp_ref[...], H_MLSTM + head)
    bb_ref[...] = lane_bcast(bs_ref[...], 3 * H_MLSTM + head)
    sf_ref[...] = jnp.zeros_like(sf_ref)
    sb_ref[...] = jnp.zeros_like(sb_ref)

    ti = lax.broadcasted_iota(jnp.int32, (CHUNK, CHUNK), 0)
    si = lax.broadcasted_iota(jnp.int32, (CHUNK, CHUNK), 1)
    mask_f = ti >= si
    mask_b = ti <= si
    sub = lax.broadcasted_iota(jnp.int32, (2 * H_MLSTM, 1), 0)

    def rowsel(x):
        return jnp.sum(jnp.where(sub == head, x, 0.0), axis=0, keepdims=True)

    mf = mb = jnp.full((1, 1), NEG, F32)
    for i in range(n_chunks):
        sf = slice(i * CHUNK, (i + 1) * CHUNK)
        sb = slice((n_chunks - 1 - i) * CHUNK, (n_chunks - i) * CHUNK)

        b = bf_ref[sf, :]
        hf, mf = _mlstm_step(qs_ref[sf, :], ks_ref[sf, :], kt_ref[:, sf], va_ref[sf, :], b,
                             rowsel(ut_ref[0:2 * H_MLSTM, sf]), mask_f, b[CHUNK - 1:CHUNK, 0:1], sf_ref, mf)
        hf_ref[sf, :] = hf

        b = bb_ref[sb, :]
        hb, mb = _mlstm_step(qs_ref[sb, :], ks_ref[sb, :], kt_ref[:, sb], va_ref[sb, :], b,
                             rowsel(ut_ref[2 * H_MLSTM:4 * H_MLSTM, sb]), mask_b, b[0:1, 0:1], sb_ref, mb)
        hb_ref[sb, :] = hb

    y = _sigmoid(o_ref[...]) * (hf_ref[...] + hb_ref[...])
    out_ref[...] = _rms(y, gain_ref[...]).astype(out_ref.dtype)


def _mlstm(proj, gates, conv_w, conv_b, gate_b, gain, *, batch, t_len):
    w = H_MLSTM * HEAD_DIM
    blk = lambda off: pl.BlockSpec((t_len, HEAD_DIM), lambda b, h: (b, off + h))
    vec = lambda off: pl.BlockSpec((1, HEAD_DIM), lambda b, h: (0, off + h))
    seq = pltpu.VMEM((t_len, HEAD_DIM), F32)
    seqb = pltpu.VMEM((t_len, HEAD_DIM), BF16)
    state = pltpu.VMEM((HEAD_DIM, 2 * HEAD_DIM), F32)
    return pl.pallas_call(
        _mlstm_kernel,
        grid=(batch, H_MLSTM),
        in_specs=[
            blk(_BLK_MQ), blk(_BLK_MK), blk(_BLK_MV), blk(_BLK_MO),
            pl.BlockSpec((t_len, LANES), lambda b, h: (b, 0)),
            pl.BlockSpec((3, HEAD_DIM), lambda b, h: (0, h)),
            pl.BlockSpec((3, HEAD_DIM), lambda b, h: (0, H_MLSTM + h)),
            vec(0), vec(H_MLSTM),
            pl.BlockSpec((1, LANES), lambda b, h: (0, 0)),
            vec(0),
        ],
        out_specs=pl.BlockSpec((t_len, HEAD_DIM), lambda b, h: (b, h)),
        out_shape=jax.ShapeDtypeStruct((batch * t_len, w), BF16),
        scratch_shapes=[
            seqb, seqb, pltpu.VMEM((HEAD_DIM, t_len), F32), pltpu.VMEM((t_len, 2 * HEAD_DIM), BF16),
            seq, seq, pltpu.VMEM((4 * H_MLSTM, t_len), F32), seq, seq, seq, seq, state, state,
        ],
        compiler_params=_params("parallel", "arbitrary"),
        name="mlstm",
    )(proj, proj, proj, proj, gates, conv_w, conv_w, conv_b.reshape(1, -1), conv_b.reshape(1, -1),
      gate_b, gain.reshape(1, -1))


def _na_kernel(q_ref, k_ref, v_ref, bias_ref, out_ref, qb_ref, kb_ref, va_ref):
    t_len = q_ref.shape[0]
    rows = t_len // GRID_W
    wr = min(WIN_R, rows)
    n_keys = wr * GRID_W

    qb_ref[...] = (q_ref[...] * (HEAD_DIM ** -0.5)).astype(BF16)
    kb_ref[...] = k_ref[...].astype(BF16)
    va_ref[:, :HEAD_DIM] = v_ref[...].astype(BF16)
    va_ref[:, HEAD_DIM:] = jnp.ones((t_len, HEAD_DIM), BF16)

    for r in range(rows):
        r0 = min(max(r - wr // 2, 0), rows - wr)
        sq = slice(r * GRID_W, (r + 1) * GRID_W)
        sk = slice(r0 * GRID_W, r0 * GRID_W + n_keys)
        s = _dot_nt(qb_ref[sq, :], kb_ref[sk, :]) + bias_ref[0, r - r0]
        e = jnp.exp(s - jnp.max(s, axis=-1, keepdims=True))
        o = _dot(e.astype(BF16), va_ref[sk, :])
        out_ref[sq, :] = (o[:, :HEAD_DIM] / o[:, HEAD_DIM:]).astype(out_ref.dtype)


def _na_bias_table(rpb, rows):
    wr = min(WIN_R, rows)
    off = np.arange(wr)[:, None, None, None]
    qc = np.arange(GRID_W)[None, :, None, None]
    kr = np.arange(wr)[None, None, :, None]
    kc = np.arange(GRID_W)[None, None, None, :]
    win_c0 = np.clip(qc - WIN_C // 2, 0, GRID_W - WIN_C)
    ok = (kc >= win_c0) & (kc < win_c0 + WIN_C)
    sel_r = (np.arange(2 * WIN_R - 1) == (kr - off + WIN_R - 1)[..., None])[:, 0, :, 0, :]
    sel_c = (np.arange(2 * WIN_C - 1) == (kc - qc + WIN_C - 1)[..., None])[0, :, 0, :, :]
    tab = jnp.einsum("okr,hrc,qjc->hoqkj", jnp.asarray(sel_r, F32), rpb.astype(F32), jnp.asarray(sel_c, F32),
                     precision=lax.Precision.HIGHEST)
    tab = jnp.where(jnp.asarray(ok), tab, NEG)
    return tab.reshape(rpb.shape[0], wr, GRID_W, wr * GRID_W)


def _na(proj, bias_tab, *, batch, t_len):
    blk = lambda off: pl.BlockSpec((t_len, HEAD_DIM), lambda b, h: (b, off + h))
    seq = pltpu.VMEM((t_len, HEAD_DIM), BF16)
    return pl.pallas_call(
        _na_kernel,
        grid=(batch, H_NA),
        in_specs=[
            blk(_BLK_NQ), blk(_BLK_NK), blk(_BLK_NV),
            pl.BlockSpec((1,) + bias_tab.shape[1:], lambda b, h: (h, 0, 0, 0)),
        ],
        out_specs=pl.BlockSpec((t_len, HEAD_DIM), lambda b, h: (b, h)),
        out_shape=jax.ShapeDtypeStruct((batch * t_len, H_NA * HEAD_DIM), BF16),
        scratch_shapes=[seq, seq, pltpu.VMEM((t_len, 2 * HEAD_DIM), BF16)],
        compiler_params=_params("parallel", "parallel"),
        name="natten",
    )(proj, proj, proj, bias_tab)


def _ret_kernel(q_ref, k_ref, v_ref, g_ref, cos_ref, sin_ref, dl_ref, gain_ref, out_ref,
                qb_ref, qx_ref, kb_ref, kz_ref, vb_ref, o_ref, kv_ref, r_ref):
    t_len = q_ref.shape[0]
    n_chunks = t_len // CHUNK

    def rope(x):
        return x * cos_ref[...] + pltpu.roll(x, HEAD_DIM // 2, 1) * sin_ref[...]

    lg = _log_sigmoid(dl_ref[0])
    lg_f, lg_b = lg[0:1, :], lg[1:2, :]
    diff = (lax.broadcasted_iota(jnp.int32, (CHUNK, CHUNK), 0)
            - lax.broadcasted_iota(jnp.int32, (CHUNK, CHUNK), 1)).astype(F32)
    intra = (jnp.where(diff >= 0, jnp.exp(jnp.where(diff >= 0, diff, 0.0) * lg_f), 0.0)
             + jnp.where(diff <= 0, jnp.exp(jnp.where(diff <= 0, -diff, 0.0) * lg_b), 0.0))
    gch_f = jnp.exp(CHUNK * lg_f)
    gch_b = jnp.exp(CHUNK * lg_b)
    pos = lax.broadcasted_iota(jnp.int32, (CHUNK, HEAD_DIM), 0).astype(F32)

    xi_f = jnp.exp((pos + 1.0) * lg_f)
    xi_b = jnp.exp((CHUNK - pos) * lg_b)
    zeta_f = jnp.exp((CHUNK - 1.0 - pos) * lg_f)
    zeta_b = jnp.exp(pos * lg_b)
    q = rope(q_ref[...])
    k = rope(k_ref[...]) * (HEAD_DIM ** -0.5)
    qb_ref[...] = q.astype(BF16)
    kb_ref[...] = k.astype(BF16)
    vb_ref[...] = v_ref[...].astype(BF16)
    for c in range(n_chunks):
        cs = slice(c * CHUNK, (c + 1) * CHUNK)
        qx_ref[cs, :HEAD_DIM] = (q[cs, :] * xi_f).astype(BF16)
        qx_ref[cs, HEAD_DIM:] = (q[cs, :] * xi_b).astype(BF16)
        kz_ref[:HEAD_DIM, cs] = (k[cs, :] * zeta_f).T.astype(BF16)
        kz_ref[HEAD_DIM:, cs] = (k[cs, :] * zeta_b).T.astype(BF16)

    def chunk(c):
        return slice(c * CHUNK, (c + 1) * CHUNK)

    for c in range(n_chunks):
        sl = chunk(c)
        vb = vb_ref[sl, :]
        s = _dot_nt(qb_ref[sl, :], kb_ref[sl, :]) * intra
        o_ref[sl, :] = _dot(s.astype(BF16), vb)
        kv_ref[c] = _dot(kz_ref[:, sl], vb)

    def state_pass(i, carry):
        rf, rb = carry
        cb = n_chunks - 1 - i
        r_ref[i, :HEAD_DIM, :] = rf.astype(BF16)
        r_ref[cb, HEAD_DIM:, :] = rb.astype(BF16)
        return gch_f * rf + kv_ref[i, :HEAD_DIM, :], gch_b * rb + kv_ref[cb, HEAD_DIM:, :]

    zero = jnp.zeros((HEAD_DIM, HEAD_DIM), F32)
    lax.fori_loop(0, n_chunks, state_pass, (zero, zero))

    for c in range(n_chunks):
        o_ref[chunk(c), :] += _dot(qx_ref[chunk(c), :], r_ref[c])

    g = g_ref[...]
    out_ref[...] = (g * _sigmoid(g) * _rms(o_ref[...], gain_ref[...])).astype(out_ref.dtype)


def _ret(proj, cos2, sin2, decay_rows, gain, *, batch, t_len):
    n_chunks = t_len // CHUNK
    blk = lambda off: pl.BlockSpec((t_len, HEAD_DIM), lambda b, h: (b, off + h))
    tab = pl.BlockSpec((t_len, HEAD_DIM), lambda b, h: (0, 0))
    seqb = pltpu.VMEM((t_len, HEAD_DIM), BF16)
    return pl.pallas_call(
        _ret_kernel,
        grid=(batch, H_RET),
        in_specs=[
            blk(_BLK_RQ), blk(_BLK_RK), blk(_BLK_RV), blk(_BLK_RG), tab, tab,
            pl.BlockSpec((1, 2, LANES), lambda b, h: (h, 0, 0)),
            pl.BlockSpec((1, HEAD_DIM), lambda b, h: (0, h)),
        ],
        out_specs=pl.BlockSpec((t_len, HEAD_DIM), lambda b, h: (b, h)),
        out_shape=jax.ShapeDtypeStruct((batch * t_len, H_RET * HEAD_DIM), BF16),
        scratch_shapes=[
            seqb, pltpu.VMEM((t_len, 2 * HEAD_DIM), BF16), seqb, pltpu.VMEM((2 * HEAD_DIM, t_len), BF16), seqb,
            pltpu.VMEM((t_len, HEAD_DIM), F32),
            pltpu.VMEM((n_chunks, 2 * HEAD_DIM, HEAD_DIM), F32),
            pltpu.VMEM((n_chunks, 2 * HEAD_DIM, HEAD_DIM), BF16),
        ],
        compiler_params=_params("parallel", "parallel"),
        name="retention",
    )(proj, proj, proj, proj, cos2, sin2, decay_rows, gain.reshape(1, -1))


def _rope_tables(t_len):
    inv = ROPE_BASE ** (-jnp.arange(0, HEAD_DIM, 2, dtype=F32) / HEAD_DIM)
    ang = jnp.arange(t_len, dtype=F32)[:, None] * inv[None, :]
    cos, sin = jnp.cos(ang), jnp.sin(ang)
    return jnp.concatenate([cos, cos], axis=-1), jnp.concatenate([-sin, sin], axis=-1)


def _pick(n, pref):
    return pref if n % pref == 0 else n


def kernel(x, ffn1_norm, ffn1_w_gu, ffn1_w_down, mix_norm, w_in, mlstm_conv_w, mlstm_conv_b, mlstm_gate_b,
           mlstm_head_norm, na_rpb, ret_decay_logit, ret_head_norm, w_out, ffn2_norm, ffn2_w_gu, ffn2_w_down,
           final_norm):
    batch, t_len, d = x.shape
    n = batch * t_len
    depth = ffn1_norm.shape[0]
    rows = t_len // GRID_W
    w_m, w_n, w_r = H_MLSTM * HEAD_DIM, H_NA * HEAD_DIM, H_RET * HEAD_DIM
    n_gates = 4 * H_MLSTM
    gate_lo = 4 * w_m

    tm = _pick(n, 1024)
    ffn = functools.partial(_ffn, tm=tm, tf=512)
    cos2, sin2 = _rope_tables(t_len)

    h = x.reshape(n, d)
    for l in range(depth):
        h = ffn(h, ffn1_norm[l], ffn1_w_gu[l].astype(BF16), ffn1_w_down[l].astype(BF16), final_norm,
                final_norm=False)

        w_main = jnp.concatenate([w_in[l][:, :gate_lo], w_in[l][:, gate_lo + n_gates:]], axis=1).astype(BF16)
        w_gate = jnp.pad(w_in[l][:, gate_lo:gate_lo + n_gates], ((0, 0), (0, LANES - n_gates))).astype(BF16)
        proj, gates = _mix_in(h, mix_norm[l], w_main, w_gate, tm=tm, tn=1024)

        gate_b = jnp.pad(mlstm_gate_b[l].astype(F32).reshape(1, n_gates), ((0, 0), (0, LANES - n_gates)))
        m_out = _mlstm(proj, gates, mlstm_conv_w[l], mlstm_conv_b[l], gate_b, mlstm_head_norm[l],
                       batch=batch, t_len=t_len)
        n_out = _na(proj, _na_bias_table(na_rpb[l], rows), batch=batch, t_len=t_len)
        decay_rows = jnp.broadcast_to(ret_decay_logit[l].astype(F32).T[:, :, None], (H_RET, 2, LANES))
        r_out = _ret(proj, cos2, sin2, decay_rows, ret_head_norm[l], batch=batch, t_len=t_len)

        wo = w_out[l].astype(BF16)
        h = _mix_out(h, m_out, n_out, r_out, wo[:w_m], wo[w_m:w_m + w_n], wo[w_m + w_n:], tm=tm, tn=1024)

        h = ffn(h, ffn2_norm[l], ffn2_w_gu[l].astype(BF16), ffn2_w_down[l].astype(BF16), final_norm,
                final_norm=(l == depth - 1))
    return h.reshape(batch, t_len, d)
```

```python
import functools

import numpy as np
import jax
import jax.numpy as jnp
from jax import lax
from jax.experimental import pallas as pl
from jax.experimental.pallas import tpu as pltpu

F32 = jnp.float32
BF16 = jnp.bfloat16

HEAD_DIM = 128
H_MLSTM = 4
H_NA = 8
H_RET = 4
CHUNK = 128
GRID_W = 64
WIN_R = 8
WIN_C = 16
ROPE_BASE = 10000.0
EPS = 1e-6
NEG = -1e30

LANES = 128
VMEM_BYTES = 64 * 1024 * 1024
VMEM_LIMIT_BYTES = VMEM_BYTES - 8 * 1024 * 1024
FFN_VMEM_LIMIT_BYTES = VMEM_BYTES - 4 * 1024 * 1024

_BLK_MQ, _BLK_MK, _BLK_MV, _BLK_MO = 0, 4, 8, 12
_BLK_NQ, _BLK_NK, _BLK_NV = 16, 24, 32
_BLK_RQ, _BLK_RK, _BLK_RV, _BLK_RG = 40, 44, 48, 52


def _params(*sem, vmem=VMEM_LIMIT_BYTES):
    return pltpu.CompilerParams(dimension_semantics=sem, vmem_limit_bytes=vmem)


def _sigmoid(x):
    return 1.0 / (1.0 + jnp.exp(-x))


def _log_sigmoid(x):
    return jnp.minimum(x, 0.0) - jnp.log1p(jnp.exp(-jnp.abs(x)))


def _rms(x, g):
    return x * lax.rsqrt(jnp.mean(x * x, axis=-1, keepdims=True) + EPS) * g


def _dot(a, b):
    return jnp.dot(a, b, preferred_element_type=F32)


def _dot_nt(a, b):
    return lax.dot_general(a, b, (((1,), (1,)), ((), ())), preferred_element_type=F32)


def _cast_kernel(w_ref, o_ref):
    o_ref[...] = w_ref[...].astype(o_ref.dtype)


def _cast_layer(w, layer, *, tr):
    _, r, c = w.shape
    return pl.pallas_call(
        _cast_kernel,
        grid=(r // tr,),
        in_specs=[pl.BlockSpec((None, tr, c), lambda i: (layer, i, 0))],
        out_specs=pl.BlockSpec((tr, c), lambda i: (i, 0)),
        out_shape=jax.ShapeDtypeStruct((r, c), BF16),
        compiler_params=_params("parallel"),
        name="cast_weight",
    )(w)


def _cast_w_in_kernel(w_ref, main_ref, gate_ref, *, gate_lo, n_gates):
    w = w_ref[...]
    main_ref[:, :gate_lo] = w[:, :gate_lo].astype(BF16)
    main_ref[:, gate_lo:] = w[:, gate_lo + n_gates:].astype(BF16)
    gate_ref[...] = jnp.zeros_like(gate_ref)
    gate_ref[:, :n_gates] = w[:, gate_lo:gate_lo + n_gates].astype(BF16)


def _cast_w_in(w_in, layer, *, gate_lo, n_gates, tr):
    _, r, c = w_in.shape
    return pl.pallas_call(
        functools.partial(_cast_w_in_kernel, gate_lo=gate_lo, n_gates=n_gates),
        grid=(r // tr,),
        in_specs=[pl.BlockSpec((None, tr, c), lambda i: (layer, i, 0))],
        out_specs=[pl.BlockSpec((tr, c - n_gates), lambda i: (i, 0)), pl.BlockSpec((tr, LANES), lambda i: (i, 0))],
        out_shape=[jax.ShapeDtypeStruct((r, c - n_gates), BF16), jax.ShapeDtypeStruct((r, LANES), BF16)],
        compiler_params=_params("parallel"),
        name="cast_w_in",
    )(w_in)


_FFN_ROWS = 256


def _ffn_kernel(x_ref, g_ref, wg_ref, wu_ref, wd_ref, fg_ref, o_ref, xn_ref, *, final_norm):
    j = pl.program_id(1)
    tm, d = o_ref.shape

    def rows(c):
        return pl.ds(pl.multiple_of(c * _FFN_ROWS, _FFN_ROWS), _FFN_ROWS)

    @pl.when(j == 0)
    def _():
        def norm_rows(c, carry):
            xn_ref[rows(c), :] = _rms(x_ref[rows(c), :], g_ref[...]).astype(BF16)
            return carry
        lax.fori_loop(0, tm // _FFN_ROWS, norm_rows, 0)
        o_ref[...] = jnp.zeros_like(o_ref)

    xn = xn_ref[...]
    gate = _dot(xn, wg_ref[...])
    up = _dot(xn, wu_ref[...])
    act = (gate * _sigmoid(gate) * up).astype(BF16)
    o_ref[...] += _dot(act, wd_ref[...])

    @pl.when(j == pl.num_programs(1) - 1)
    def _():
        def finish_rows(c, carry):
            h = x_ref[rows(c), :] + 0.5 * o_ref[rows(c), :]
            if final_norm:
                h = _rms(h, fg_ref[...])
            o_ref[rows(c), :] = h
            return carry
        lax.fori_loop(0, tm // _FFN_ROWS, finish_rows, 0)


def _ffn(h, g, w_gu, w_down, final_g, *, final_norm, tm, tf):
    n, d = h.shape
    f = w_down.shape[0]
    nf = f // tf
    return pl.pallas_call(
        functools.partial(_ffn_kernel, final_norm=final_norm),
        grid=(n // tm, nf),
        in_specs=[
            pl.BlockSpec((tm, d), lambda i, j: (i, 0)),
            pl.BlockSpec((1, d), lambda i, j: (0, 0)),
            pl.BlockSpec((d, tf), lambda i, j: (0, j)),
            pl.BlockSpec((d, tf), lambda i, j: (0, j + nf)),
            pl.BlockSpec((tf, d), lambda i, j: (j, 0)),
            pl.BlockSpec((1, d), lambda i, j: (0, 0)),
        ],
        out_specs=pl.BlockSpec((tm, d), lambda i, j: (i, 0)),
        out_shape=jax.ShapeDtypeStruct((n, d), F32),
        scratch_shapes=[pltpu.VMEM((tm, d), BF16)],
        compiler_params=_params("parallel", "arbitrary", vmem=FFN_VMEM_LIMIT_BYTES),
        name="ffn",
    )(h, g.reshape(1, d), w_gu, w_gu, w_down, final_g.reshape(1, d))


def _mix_in_kernel(x_ref, g_ref, w_ref, wgate_ref, o_ref, og_ref, xn_ref):
    @pl.when(pl.program_id(1) == 0)
    def _():
        xn = _rms(x_ref[...], g_ref[...]).astype(BF16)
        xn_ref[...] = xn
        og_ref[...] = _dot(xn, wgate_ref[...])

    o_ref[...] = _dot(xn_ref[...], w_ref[...])


def _mix_in(h, g, w_main, w_gate, *, tm, tn):
    n, d = h.shape
    c = w_main.shape[1]
    return pl.pallas_call(
        _mix_in_kernel,
        grid=(n // tm, c // tn),
        in_specs=[
            pl.BlockSpec((tm, d), lambda i, j: (i, 0)),
            pl.BlockSpec((1, d), lambda i, j: (0, 0)),
            pl.BlockSpec((d, tn), lambda i, j: (0, j)),
            pl.BlockSpec((d, LANES), lambda i, j: (0, 0)),
        ],
        out_specs=[
            pl.BlockSpec((tm, tn), lambda i, j: (i, j)),
            pl.BlockSpec((tm, LANES), lambda i, j: (i, 0)),
        ],
        out_shape=[jax.ShapeDtypeStruct((n, c), F32), jax.ShapeDtypeStruct((n, LANES), F32)],
        scratch_shapes=[pltpu.VMEM((tm, d), BF16)],
        compiler_params=_params("parallel", "arbitrary"),
        name="mix_in",
    )(h, g.reshape(1, d), w_main, w_gate)


def _mix_out_kernel(h_ref, m_ref, na_ref, nb_ref, r_ref, wm_ref, wna_ref, wnb_ref, wr_ref, o_ref):
    acc = (_dot(m_ref[...], wm_ref[...]) + _dot(na_ref[...], wna_ref[...])
           + _dot(nb_ref[...], wnb_ref[...]) + _dot(r_ref[...], wr_ref[...]))
    o_ref[...] = h_ref[...] + acc


def _mix_out(h, m_out, n_out, r_out, w_out, *, tm, tn):
    n, d = h.shape
    kb = m_out.shape[1]
    assert n_out.shape[1] == 2 * kb and r_out.shape[1] == kb and w_out.shape[0] == 4 * kb
    act = lambda c: pl.BlockSpec((tm, kb), lambda i, j: (i, c))
    wblk = lambda r: pl.BlockSpec((kb, tn), lambda i, j: (r, j))
    return pl.pallas_call(
        _mix_out_kernel,
        grid=(n // tm, d // tn),
        in_specs=[pl.BlockSpec((tm, tn), lambda i, j: (i, j)), act(0), act(0), act(1), act(0),
                  wblk(0), wblk(1), wblk(2), wblk(3)],
        out_specs=pl.BlockSpec((tm, tn), lambda i, j: (i, j)),
        out_shape=jax.ShapeDtypeStruct((n, d), F32),
        compiler_params=_params("parallel", "arbitrary"),
        name="mix_out",
    )(h, m_out, n_out, n_out, r_out, w_out, w_out, w_out, w_out)


def _mlstm_step(qb, kb, kt, va, b, u_row, mask, g, st_ref, m_st):
    dmat = jnp.where(mask, b + u_row, NEG)
    inter = b + m_st
    m_t = jnp.maximum(inter, jnp.max(dmat, axis=-1, keepdims=True))
    s = _dot_nt(qb, kb) * jnp.exp(dmat - m_t)
    st = st_ref[...]
    r_intra = _dot(s.astype(BF16), va)
    r_inter = _dot(qb, st.astype(BF16))
    w_inter = jnp.exp(inter - m_t)
    num = r_intra[:, :HEAD_DIM] + w_inter * r_inter[:, :HEAD_DIM]
    den = r_intra[:, HEAD_DIM:] + w_inter * r_inter[:, HEAD_DIM:]
    h_out = num / jnp.maximum(jnp.abs(den), jnp.exp(-m_t))
    a = g + u_row
    m_new = jnp.maximum(g + m_st, jnp.max(a, axis=-1, keepdims=True))
    ktw = (kt * jnp.exp(a - m_new)).astype(BF16)
    st_ref[...] = jnp.exp(g + m_st - m_new) * st + _dot(ktw, va)
    return h_out, m_new


def _mlstm_kernel(q_ref, k_ref, v_ref, o_ref, gt_ref, cwq_ref, cwk_ref, cbq_ref, cbk_ref, gb_ref, gain_ref,
                  out_ref, qs_ref, ks_ref, kt_ref, va_ref, bp_ref, bs_ref, ut_ref, bf_ref, bb_ref, hf_ref, hb_ref,
                  sf_ref, sb_ref):
    t_len = q_ref.shape[0]
    n_chunks = t_len // CHUNK
    head = pl.program_id(1)
    row = lax.broadcasted_iota(jnp.int32, (t_len, 1), 0)
    lane = lax.broadcasted_iota(jnp.int32, (1, LANES), 1)

    @pl.when(head == 0)
    def _():
        gates = gt_ref[...] + gb_ref[...]
        logf = _log_sigmoid(gates)
        rin = row & (CHUNK - 1)
        bp = logf
        bs = logf
        k = 1
        while k < CHUNK:
            bp = bp + jnp.where(rin >= k, pltpu.roll(bp, k, 0), 0.0)
            bs = bs + jnp.where(rin < CHUNK - k, pltpu.roll(bs, t_len - k, 0), 0.0)
            k *= 2
        bp_ref[...] = bp
        bs_ref[...] = bs
        u = gates - jnp.where(lane < 2 * H_MLSTM, pltpu.roll(bp, LANES - H_MLSTM, 1),
                              pltpu.roll(bs, LANES - H_MLSTM, 1))
        for c in range(n_chunks):
            ut_ref[:, c * CHUNK:(c + 1) * CHUNK] = u[c * CHUNK:(c + 1) * CHUNK, :].T[0:ut_ref.shape[0], :]

    def conv_silu(x, w_ref, b_ref):
        prev = jnp.where(row >= 1, pltpu.roll(x, 1, 0), 0.0)
        nxt = jnp.where(row <= t_len - 2, pltpu.roll(x, t_len - 1, 0), 0.0)
        y = prev * w_ref[0:1, :] + x * w_ref[1:2, :] + nxt * w_ref[2:3, :] + b_ref[...]
        return y * _sigmoid(y)

    def lane_bcast(x, idx):
        return jnp.broadcast_to(jnp.sum(jnp.where(lane == idx, x, 0.0), axis=-1, keepdims=True), x.shape)

    qs_ref[...] = (conv_silu(q_ref[...], cwq_ref, cbq_ref) * (HEAD_DIM ** -0.5)).astype(BF16)
    k = conv_silu(k_ref[...], cwk_ref, cbk_ref)
    ks_ref[...] = k.astype(BF16)
    for c in range(n_chunks):
        kt_ref[:, c * CHUNK:(c + 1) * CHUNK] = k[c * CHUNK:(c + 1) * CHUNK, :].T
    va_ref[:, :HEAD_DIM] = v_ref[...].astype(BF16)
    va_ref[:, HEAD_DIM:] = jnp.ones((t_len, HEAD_DIM), BF16)
    bf_ref[...] = lane_bcast(bp_ref[...], H_MLSTM + head)
    bb_ref[...] = lane_bcast(bs_ref[...], 3 * H_MLSTM + head)
    sf_ref[...] = jnp.zeros_like(sf_ref)
    sb_ref[...] = jnp.zeros_like(sb_ref)

    ti = lax.broadcasted_iota(jnp.int32, (CHUNK, CHUNK), 0)
    si = lax.broadcasted_iota(jnp.int32, (CHUNK, CHUNK), 1)
    mask_f = ti >= si
    mask_b = ti <= si
    sub = lax.broadcasted_iota(jnp.int32, (2 * H_MLSTM, 1), 0)

    def rowsel(x):
        return jnp.sum(jnp.where(sub == head, x, 0.0), axis=0, keepdims=True)

    mf = mb = jnp.full((1, 1), NEG, F32)
    for i in range(n_chunks):
        sf = slice(i * CHUNK, (i + 1) * CHUNK)
        sb = slice((n_chunks - 1 - i) * CHUNK, (n_chunks - i) * CHUNK)

        b = bf_ref[sf, :]
        hf, mf = _mlstm_step(qs_ref[sf, :], ks_ref[sf, :], kt_ref[:, sf], va_ref[sf, :], b,
                             rowsel(ut_ref[0:2 * H_MLSTM, sf]), mask_f, b[CHUNK - 1:CHUNK, 0:1], sf_ref, mf)
        hf_ref[sf, :] = hf

        b = bb_ref[sb, :]
        hb, mb = _mlstm_step(qs_ref[sb, :], ks_ref[sb, :], kt_ref[:, sb], va_ref[sb, :], b,
                             rowsel(ut_ref[2 * H_MLSTM:4 * H_MLSTM, sb]), mask_b, b[0:1, 0:1], sb_ref, mb)
        hb_ref[sb, :] = hb

    y = _sigmoid(o_ref[...]) * (hf_ref[...] + hb_ref[...])
    out_ref[...] = _rms(y, gain_ref[...]).astype(out_ref.dtype)


def _mlstm(proj, gates, conv_w, conv_b, gate_b, gain, *, batch, t_len):
    w = H_MLSTM * HEAD_DIM
    blk = lambda off: pl.BlockSpec((t_len, HEAD_DIM), lambda b, h: (b, off + h))
    vec = lambda off: pl.BlockSpec((1, HEAD_DIM), lambda b, h: (0, off + h))
    seq = pltpu.VMEM((t_len, HEAD_DIM), F32)
    seqb = pltpu.VMEM((t_len, HEAD_DIM), BF16)
    state = pltpu.VMEM((HEAD_DIM, 2 * HEAD_DIM), F32)
    return pl.pallas_call(
        _mlstm_kernel,
        grid=(batch, H_MLSTM),
        in_specs=[
            blk(_BLK_MQ), blk(_BLK_MK), blk(_BLK_MV), blk(_BLK_MO),
            pl.BlockSpec((t_len, LANES), lambda b, h: (b, 0)),
            pl.BlockSpec((3, HEAD_DIM), lambda b, h: (0, h)),
            pl.BlockSpec((3, HEAD_DIM), lambda b, h: (0, H_MLSTM + h)),
            vec(0), vec(H_MLSTM),
            pl.BlockSpec((1, LANES), lambda b, h: (0, 0)),
            vec(0),
        ],
        out_specs=pl.BlockSpec((t_len, HEAD_DIM), lambda b, h: (b, h)),
        out_shape=jax.ShapeDtypeStruct((batch * t_len, w), BF16),
        scratch_shapes=[
            seqb, seqb, pltpu.VMEM((HEAD_DIM, t_len), F32), pltpu.VMEM((t_len, 2 * HEAD_DIM), BF16),
            seq, seq, pltpu.VMEM((4 * H_MLSTM, t_len), F32), seq, seq, seq, seq, state, state,
        ],
        compiler_params=_params("parallel", "arbitrary"),
        name="mlstm",
    )(proj, proj, proj, proj, gates, conv_w, conv_w, conv_b.reshape(1, -1), conv_b.reshape(1, -1),
      gate_b, gain.reshape(1, -1))


def _na_kernel(q_ref, k_ref, v_ref, bias_ref, out_ref, qb_ref, kb_ref, va_ref):
    t_len = q_ref.shape[0]
    rows = t_len // GRID_W
    wr = min(WIN_R, rows)
    n_keys = wr * GRID_W

    qb_ref[...] = (q_ref[...] * (HEAD_DIM ** -0.5)).astype(BF16)
    kb_ref[...] = k_ref[...].astype(BF16)
    va_ref[:, :HEAD_DIM] = v_ref[...].astype(BF16)
    va_ref[:, HEAD_DIM:] = jnp.ones((t_len, HEAD_DIM), BF16)

    for r in range(rows):
        r0 = min(max(r - wr // 2, 0), rows - wr)
        sq = slice(r * GRID_W, (r + 1) * GRID_W)
        sk = slice(r0 * GRID_W, r0 * GRID_W + n_keys)
        s = _dot_nt(qb_ref[sq, :], kb_ref[sk, :]) + bias_ref[0, r - r0]
        e = jnp.exp(s - jnp.max(s, axis=-1, keepdims=True))
        o = _dot(e.astype(BF16), va_ref[sk, :])
        out_ref[sq, :] = (o[:, :HEAD_DIM] / o[:, HEAD_DIM:]).astype(out_ref.dtype)


def _na_bias_table(rpb, rows):
    wr = min(WIN_R, rows)
    off = np.arange(wr)[:, None, None, None]
    qc = np.arange(GRID_W)[None, :, None, None]
    kr = np.arange(wr)[None, None, :, None]
    kc = np.arange(GRID_W)[None, None, None, :]
    win_c0 = np.clip(qc - WIN_C // 2, 0, GRID_W - WIN_C)
    ok = (kc >= win_c0) & (kc < win_c0 + WIN_C)
    sel_r = (np.arange(2 * WIN_R - 1) == (kr - off + WIN_R - 1)[..., None])[:, 0, :, 0, :]
    sel_c = (np.arange(2 * WIN_C - 1) == (kc - qc + WIN_C - 1)[..., None])[0, :, 0, :, :]
    tab = jnp.einsum("okr,hrc,qjc->hoqkj", jnp.asarray(sel_r, F32), rpb.astype(F32), jnp.asarray(sel_c, F32),
                     precision=lax.Precision.HIGHEST)
    tab = jnp.where(jnp.asarray(ok), tab, NEG)
    return tab.reshape(rpb.shape[0], wr, GRID_W, wr * GRID_W)


def _na(proj, bias_tab, *, batch, t_len):
    blk = lambda off: pl.BlockSpec((t_len, HEAD_DIM), lambda b, h: (b, off + h))
    seq = pltpu.VMEM((t_len, HEAD_DIM), BF16)
    return pl.pallas_call(
        _na_kernel,
        grid=(batch, H_NA),
        in_specs=[
            blk(_BLK_NQ), blk(_BLK_NK), blk(_BLK_NV),
            pl.BlockSpec((1,) + bias_tab.shape[1:], lambda b, h: (h, 0, 0, 0)),
        ],
        out_specs=pl.BlockSpec((t_len, HEAD_DIM), lambda b, h: (b, h)),
        out_shape=jax.ShapeDtypeStruct((batch * t_len, H_NA * HEAD_DIM), BF16),
        scratch_shapes=[seq, seq, pltpu.VMEM((t_len, 2 * HEAD_DIM), BF16)],
        compiler_params=_params("parallel", "parallel"),
        name="natten",
    )(proj, proj, proj, bias_tab)


def _ret_kernel(q_ref, k_ref, v_ref, g_ref, cos_ref, sin_ref, dl_ref, gain_ref, out_ref,
                qb_ref, qx_ref, kb_ref, kz_ref, vb_ref, o_ref, kv_ref, r_ref):
    t_len = q_ref.shape[0]
    n_chunks = t_len // CHUNK

    def rope(x):
        return x * cos_ref[...] + pltpu.roll(x, HEAD_DIM // 2, 1) * sin_ref[...]

    lg = _log_sigmoid(dl_ref[0])
    lg_f, lg_b = lg[0:1, :], lg[1:2, :]
    diff = (lax.broadcasted_iota(jnp.int32, (CHUNK, CHUNK), 0)
            - lax.broadcasted_iota(jnp.int32, (CHUNK, CHUNK), 1)).astype(F32)
    intra = (jnp.where(diff >= 0, jnp.exp(jnp.where(diff >= 0, diff, 0.0) * lg_f), 0.0)
             + jnp.where(diff <= 0, jnp.exp(jnp.where(diff <= 0, -diff, 0.0) * lg_b), 0.0))
    gch_f = jnp.exp(CHUNK * lg_f)
    gch_b = jnp.exp(CHUNK * lg_b)
    pos = lax.broadcasted_iota(jnp.int32, (CHUNK, HEAD_DIM), 0).astype(F32)

    xi_f = jnp.exp((pos + 1.0) * lg_f)
    xi_b = jnp.exp((CHUNK - pos) * lg_b)
    zeta_f = jnp.exp((CHUNK - 1.0 - pos) * lg_f)
    zeta_b = jnp.exp(pos * lg_b)
    q = rope(q_ref[...])
    k = rope(k_ref[...]) * (HEAD_DIM ** -0.5)
    qb_ref[...] = q.astype(BF16)
    kb_ref[...] = k.astype(BF16)
    vb_ref[...] = v_ref[...].astype(BF16)
    for c in range(n_chunks):
        cs = slice(c * CHUNK, (c + 1) * CHUNK)
        qx_ref[cs, :HEAD_DIM] = (q[cs, :] * xi_f).astype(BF16)
        qx_ref[cs, HEAD_DIM:] = (q[cs, :] * xi_b).astype(BF16)
        kz_ref[:HEAD_DIM, cs] = (k[cs, :] * zeta_f).T.astype(BF16)
        kz_ref[HEAD_DIM:, cs] = (k[cs, :] * zeta_b).T.astype(BF16)

    def chunk(c):
        return slice(c * CHUNK, (c + 1) * CHUNK)

    for c in range(n_chunks):
        sl = chunk(c)
        vb = vb_ref[sl, :]
        s = _dot_nt(qb_ref[sl, :], kb_ref[sl, :]) * intra
        o_ref[sl, :] = _dot(s.astype(BF16), vb)
        kv_ref[c] = _dot(kz_ref[:, sl], vb)

    def state_pass(i, carry):
        rf, rb = carry
        cb = n_chunks - 1 - i
        r_ref[i, :HEAD_DIM, :] = rf.astype(BF16)
        r_ref[cb, HEAD_DIM:, :] = rb.astype(BF16)
        return gch_f * rf + kv_ref[i, :HEAD_DIM, :], gch_b * rb + kv_ref[cb, HEAD_DIM:, :]

    zero = jnp.zeros((HEAD_DIM, HEAD_DIM), F32)
    lax.fori_loop(0, n_chunks, state_pass, (zero, zero))

    for c in range(n_chunks):
        o_ref[chunk(c), :] += _dot(qx_ref[chunk(c), :], r_ref[c])

    g = g_ref[...]
    out_ref[...] = (g * _sigmoid(g) * _rms(o_ref[...], gain_ref[...])).astype(out_ref.dtype)


def _ret(proj, cos2, sin2, decay_rows, gain, *, batch, t_len):
    n_chunks = t_len // CHUNK
    blk = lambda off: pl.BlockSpec((t_len, HEAD_DIM), lambda b, h: (b, off + h))
    tab = pl.BlockSpec((t_len, HEAD_DIM), lambda b, h: (0, 0))
    seqb = pltpu.VMEM((t_len, HEAD_DIM), BF16)
    return pl.pallas_call(
        _ret_kernel,
        grid=(batch, H_RET),
        in_specs=[
            blk(_BLK_RQ), blk(_BLK_RK), blk(_BLK_RV), blk(_BLK_RG), tab, tab,
            pl.BlockSpec((1, 2, LANES), lambda b, h: (h, 0, 0)),
            pl.BlockSpec((1, HEAD_DIM), lambda b, h: (0, h)),
        ],
        out_specs=pl.BlockSpec((t_len, HEAD_DIM), lambda b, h: (b, h)),
        out_shape=jax.ShapeDtypeStruct((batch * t_len, H_RET * HEAD_DIM), BF16),
        scratch_shapes=[
            seqb, pltpu.VMEM((t_len, 2 * HEAD_DIM), BF16), seqb, pltpu.VMEM((2 * HEAD_DIM, t_len), BF16), seqb,
            pltpu.VMEM((t_len, HEAD_DIM), F32),
            pltpu.VMEM((n_chunks, 2 * HEAD_DIM, HEAD_DIM), F32),
            pltpu.VMEM((n_chunks, 2 * HEAD_DIM, HEAD_DIM), BF16),
        ],
        compiler_params=_params("parallel", "parallel"),
        name="retention",
    )(proj, proj, proj, proj, cos2, sin2, decay_rows, gain.reshape(1, -1))


def _rope_tables(t_len):
    inv = ROPE_BASE ** (-jnp.arange(0, HEAD_DIM, 2, dtype=F32) / HEAD_DIM)
    ang = jnp.arange(t_len, dtype=F32)[:, None] * inv[None, :]
    cos, sin = jnp.cos(ang), jnp.sin(ang)
    return jnp.concatenate([cos, cos], axis=-1), jnp.concatenate([-sin, sin], axis=-1)


def _pick(n, pref):
    return pref if n % pref == 0 else n


def kernel(x, ffn1_norm, ffn1_w_gu, ffn1_w_down, mix_norm, w_in, mlstm_conv_w, mlstm_conv_b, mlstm_gate_b,
           mlstm_head_norm, na_rpb, ret_decay_logit, ret_head_norm, w_out, ffn2_norm, ffn2_w_gu, ffn2_w_down,
           final_norm):
    batch, t_len, d = x.shape
    n = batch * t_len
    depth = ffn1_norm.shape[0]
    rows = t_len // GRID_W
    n_gates = 4 * H_MLSTM
    gate_lo = 4 * H_MLSTM * HEAD_DIM

    tm = _pick(n, 1024)
    ffn = functools.partial(_ffn, tm=tm, tf=512)
    cos2, sin2 = _rope_tables(t_len)

    h = x.reshape(n, d)
    for l in range(depth):
        h = ffn(h, ffn1_norm[l], _cast_layer(ffn1_w_gu, l, tr=256), _cast_layer(ffn1_w_down, l, tr=704),
                final_norm, final_norm=False)

        w_main, w_gate = _cast_w_in(w_in, l, gate_lo=gate_lo, n_gates=n_gates, tr=256)
        proj, gates = _mix_in(h, mix_norm[l], w_main, w_gate, tm=tm, tn=1792)

        gate_b = jnp.pad(mlstm_gate_b[l].astype(F32).reshape(1, n_gates), ((0, 0), (0, LANES - n_gates)))
        m_out = _mlstm(proj, gates, mlstm_conv_w[l], mlstm_conv_b[l], gate_b, mlstm_head_norm[l],
                       batch=batch, t_len=t_len)
        n_out = _na(proj, _na_bias_table(na_rpb[l], rows), batch=batch, t_len=t_len)
        decay_rows = jnp.broadcast_to(ret_decay_logit[l].astype(F32).T[:, :, None], (H_RET, 2, LANES))
        r_out = _ret(proj, cos2, sin2, decay_rows, ret_head_norm[l], batch=batch, t_len=t_len)

        h = _mix_out(h, m_out, n_out, r_out, _cast_layer(w_out, l, tr=512), tm=tm, tn=1024)

        h = ffn(h, ffn2_norm[l], _cast_layer(ffn2_w_gu, l, tr=256), _cast_layer(ffn2_w_down, l, tr=704),
                final_norm, final_norm=(l == depth - 1))
    return h.reshape(batch, t_len, d)
```

```python
import functools

import numpy as np
import jax
import jax.numpy as jnp
from jax import lax
from jax.experimental import pallas as pl
from jax.experimental.pallas import tpu as pltpu

F32 = jnp.float32
BF16 = jnp.bfloat16

HEAD_DIM = 128
H_MLSTM = 4
H_NA = 8
H_RET = 4
CHUNK = 128
GRID_W = 64
WIN_R = 8
WIN_C = 16
ROPE_BASE = 10000.0
EPS = 1e-6
NEG = -1e30

LANES = 128
VMEM_BYTES = 64 * 1024 * 1024
VMEM_LIMIT_BYTES = VMEM_BYTES - 8 * 1024 * 1024
FFN_VMEM_LIMIT_BYTES = VMEM_BYTES - 4 * 1024 * 1024

_BLK_MQ, _BLK_MK, _BLK_MV, _BLK_MO = 0, 4, 8, 12
_BLK_NQ, _BLK_NK, _BLK_NV = 16, 24, 32
_BLK_RQ, _BLK_RK, _BLK_RV, _BLK_RG = 40, 44, 48, 52


def _params(*sem, vmem=VMEM_LIMIT_BYTES):
    return pltpu.CompilerParams(dimension_semantics=sem, vmem_limit_bytes=vmem)


def _sigmoid(x):
    return 1.0 / (1.0 + jnp.exp(-x))


def _log_sigmoid(x):
    return jnp.minimum(x, 0.0) - jnp.log1p(jnp.exp(-jnp.abs(x)))


def _rms(x, g):
    return x * lax.rsqrt(jnp.mean(x * x, axis=-1, keepdims=True) + EPS) * g


def _dot(a, b):
    return jnp.dot(a, b, preferred_element_type=F32)


def _dot_nt(a, b):
    return lax.dot_general(a, b, (((1,), (1,)), ((), ())), preferred_element_type=F32)


def _cast_kernel(w_ref, o_ref):
    o_ref[...] = w_ref[...].astype(o_ref.dtype)


def _cast_layer(w, layer, *, tr):
    _, r, c = w.shape
    return pl.pallas_call(
        _cast_kernel,
        grid=(r // tr,),
        in_specs=[pl.BlockSpec((None, tr, c), lambda i: (layer, i, 0))],
        out_specs=pl.BlockSpec((tr, c), lambda i: (i, 0)),
        out_shape=jax.ShapeDtypeStruct((r, c), BF16),
        compiler_params=_params("parallel"),
        name="cast_weight",
    )(w)


def _cast_w_in_kernel(wt_ref, gt_ref, main_ref, gate_ref, *, n_gates):
    main_ref[...] = wt_ref[0].T.astype(BF16)

    @pl.when(pl.program_id(0) == 0)
    def _():
        gate_ref[...] = jnp.zeros_like(gate_ref)
        gate_ref[:, :n_gates] = gt_ref[0].T.astype(BF16)


def _cast_w_in(w_in, layer, *, gate_lo, n_gates, tc):
    wt = jnp.swapaxes(w_in, 1, 2)
    _, c, d = wt.shape
    n_main = c - n_gates

    def src_rows(i):
        return pl.multiple_of(jnp.where(i * tc >= gate_lo, i * tc + n_gates, i * tc), n_gates)

    return pl.pallas_call(
        functools.partial(_cast_w_in_kernel, n_gates=n_gates),
        grid=(n_main // tc,),
        in_specs=[
            pl.BlockSpec((pl.Element(1), pl.Element(tc), pl.Element(d)), lambda i: (layer, src_rows(i), 0)),
            pl.BlockSpec((pl.Element(1), pl.Element(n_gates), pl.Element(d)), lambda i: (layer, gate_lo, 0)),
        ],
        out_specs=[pl.BlockSpec((d, tc), lambda i: (0, i)), pl.BlockSpec((d, LANES), lambda i: (0, 0))],
        out_shape=[jax.ShapeDtypeStruct((d, n_main), BF16), jax.ShapeDtypeStruct((d, LANES), BF16)],
        compiler_params=_params("arbitrary"),
        name="cast_w_in",
    )(wt, wt)


_FFN_ROWS = 256


def _ffn_kernel(x_ref, g_ref, wg_ref, wu_ref, wd_ref, fg_ref, o_ref, xn_ref, *, final_norm):
    j = pl.program_id(1)
    tm, d = o_ref.shape

    def rows(c):
        return pl.ds(pl.multiple_of(c * _FFN_ROWS, _FFN_ROWS), _FFN_ROWS)

    @pl.when(j == 0)
    def _():
        def norm_rows(c, carry):
            xn_ref[rows(c), :] = _rms(x_ref[rows(c), :], g_ref[...]).astype(BF16)
            return carry
        lax.fori_loop(0, tm // _FFN_ROWS, norm_rows, 0)
        o_ref[...] = jnp.zeros_like(o_ref)

    xn = xn_ref[...]
    gate = _dot(xn, wg_ref[...])
    up = _dot(xn, wu_ref[...])
    act = (gate * _sigmoid(gate) * up).astype(BF16)
    o_ref[...] += _dot(act, wd_ref[...])

    @pl.when(j == pl.num_programs(1) - 1)
    def _():
        def finish_rows(c, carry):
            h = x_ref[rows(c), :] + 0.5 * o_ref[rows(c), :]
            if final_norm:
                h = _rms(h, fg_ref[...])
            o_ref[rows(c), :] = h
            return carry
        lax.fori_loop(0, tm // _FFN_ROWS, finish_rows, 0)


def _ffn(h, g, w_gu, w_down, final_g, *, final_norm, tm, tf):
    n, d = h.shape
    f = w_down.shape[0]
    nf = f // tf
    return pl.pallas_call(
        functools.partial(_ffn_kernel, final_norm=final_norm),
        grid=(n // tm, nf),
        in_specs=[
            pl.BlockSpec((tm, d), lambda i, j: (i, 0)),
            pl.BlockSpec((1, d), lambda i, j: (0, 0)),
            pl.BlockSpec((d, tf), lambda i, j: (0, j)),
            pl.BlockSpec((d, tf), lambda i, j: (0, j + nf)),
            pl.BlockSpec((tf, d), lambda i, j: (j, 0)),
            pl.BlockSpec((1, d), lambda i, j: (0, 0)),
        ],
        out_specs=pl.BlockSpec((tm, d), lambda i, j: (i, 0)),
        out_shape=jax.ShapeDtypeStruct((n, d), F32),
        scratch_shapes=[pltpu.VMEM((tm, d), BF16)],
        compiler_params=_params("parallel", "arbitrary", vmem=FFN_VMEM_LIMIT_BYTES),
        name="ffn",
    )(h, g.reshape(1, d), w_gu, w_gu, w_down, final_g.reshape(1, d))


def _mix_in_kernel(x_ref, g_ref, w_ref, wgate_ref, o_ref, og_ref, xn_ref):
    @pl.when(pl.program_id(1) == 0)
    def _():
        xn = _rms(x_ref[...], g_ref[...]).astype(BF16)
        xn_ref[...] = xn
        og_ref[...] = _dot(xn, wgate_ref[...])

    res = _dot(xn_ref[...], w_ref[...])
    for k in range(o_ref.shape[0]):
        o_ref[k] = res[:, k * HEAD_DIM:(k + 1) * HEAD_DIM]


def _mix_in(h, g, w_main, w_gate, *, tm, tn):
    n, d = h.shape
    c = w_main.shape[1]
    nb = tn // HEAD_DIM
    return pl.pallas_call(
        _mix_in_kernel,
        grid=(n // tm, c // tn),
        in_specs=[
            pl.BlockSpec((tm, d), lambda i, j: (i, 0)),
            pl.BlockSpec((1, d), lambda i, j: (0, 0)),
            pl.BlockSpec((d, tn), lambda i, j: (0, j)),
            pl.BlockSpec((d, LANES), lambda i, j: (0, 0)),
        ],
        out_specs=[
            pl.BlockSpec((nb, tm, HEAD_DIM), lambda i, j: (j, i, 0)),
            pl.BlockSpec((tm, LANES), lambda i, j: (i, 0)),
        ],
        out_shape=[jax.ShapeDtypeStruct((c // HEAD_DIM, n, HEAD_DIM), F32), jax.ShapeDtypeStruct((n, LANES), F32)],
        scratch_shapes=[pltpu.VMEM((tm, d), BF16)],
        compiler_params=_params("parallel", "arbitrary"),
        name="mix_in",
    )(h, g.reshape(1, d), w_main, w_gate)


def _mix_out_kernel(h_ref, m_ref, na_ref, nb_ref, r_ref, wm_ref, wna_ref, wnb_ref, wr_ref, o_ref):
    acc = (_dot(m_ref[...], wm_ref[...]) + _dot(na_ref[...], wna_ref[...])
           + _dot(nb_ref[...], wnb_ref[...]) + _dot(r_ref[...], wr_ref[...]))
    o_ref[...] = h_ref[...] + acc


def _mix_out(h, m_out, n_out, r_out, w_out, *, tm, tn):
    n, d = h.shape
    kb = m_out.shape[1]
    assert n_out.shape[1] == 2 * kb and r_out.shape[1] == kb and w_out.shape[0] == 4 * kb
    act = lambda c: pl.BlockSpec((tm, kb), lambda i, j: (i, c))
    wblk = lambda r: pl.BlockSpec((kb, tn), lambda i, j: (r, j))
    return pl.pallas_call(
        _mix_out_kernel,
        grid=(n // tm, d // tn),
        in_specs=[pl.BlockSpec((tm, tn), lambda i, j: (i, j)), act(0), act(0), act(1), act(0),
                  wblk(0), wblk(1), wblk(2), wblk(3)],
        out_specs=pl.BlockSpec((tm, tn), lambda i, j: (i, j)),
        out_shape=jax.ShapeDtypeStruct((n, d), F32),
        compiler_params=_params("parallel", "arbitrary"),
        name="mix_out",
    )(h, m_out, n_out, n_out, r_out, w_out, w_out, w_out, w_out)


def _mlstm_step(qb, kb, kt, va, b, u_row, mask, g, st_ref, m_st):
    dmat = jnp.where(mask, b + u_row, NEG)
    inter = b + m_st
    m_t = jnp.maximum(inter, jnp.max(dmat, axis=-1, keepdims=True))
    s = _dot_nt(qb, kb) * jnp.exp(dmat - m_t)
    st = st_ref[...]
    r_intra = _dot(s.astype(BF16), va)
    r_inter = _dot(qb, st.astype(BF16))
    w_inter = jnp.exp(inter - m_t)
    num = r_intra[:, :HEAD_DIM] + w_inter * r_inter[:, :HEAD_DIM]
    den = r_intra[:, HEAD_DIM:] + w_inter * r_inter[:, HEAD_DIM:]
    h_out = num / jnp.maximum(jnp.abs(den), jnp.exp(-m_t))
    a = g + u_row
    m_new = jnp.maximum(g + m_st, jnp.max(a, axis=-1, keepdims=True))
    ktw = (kt * jnp.exp(a - m_new)).astype(BF16)
    st_ref[...] = jnp.exp(g + m_st - m_new) * st + _dot(ktw, va)
    return h_out, m_new


def _mlstm_kernel(q_ref, k_ref, v_ref, o_ref, gt_ref, cwq_ref, cwk_ref, cbq_ref, cbk_ref, gb_ref, gain_ref,
                  out_ref, qs_ref, ks_ref, kt_ref, va_ref, bp_ref, bs_ref, ut_ref, bf_ref, bb_ref, hf_ref, hb_ref,
                  sf_ref, sb_ref):
    t_len = q_ref.shape[0]
    n_chunks = t_len // CHUNK
    head = pl.program_id(1)
    row = lax.broadcasted_iota(jnp.int32, (t_len, 1), 0)
    lane = lax.broadcasted_iota(jnp.int32, (1, LANES), 1)

    @pl.when(head == 0)
    def _():
        gates = gt_ref[...] + gb_ref[...]
        logf = _log_sigmoid(gates)
        rin = row & (CHUNK - 1)
        bp = logf
        bs = logf
        k = 1
        while k < CHUNK:
            bp = bp + jnp.where(rin >= k, pltpu.roll(bp, k, 0), 0.0)
            bs = bs + jnp.where(rin < CHUNK - k, pltpu.roll(bs, t_len - k, 0), 0.0)
            k *= 2
        bp_ref[...] = bp
        bs_ref[...] = bs
        u = gates - jnp.where(lane < 2 * H_MLSTM, pltpu.roll(bp, LANES - H_MLSTM, 1),
                              pltpu.roll(bs, LANES - H_MLSTM, 1))
        for c in range(n_chunks):
            ut_ref[:, c * CHUNK:(c + 1) * CHUNK] = u[c * CHUNK:(c + 1) * CHUNK, :].T[0:ut_ref.shape[0], :]

    def conv_silu(x, w_ref, b_ref):
        prev = jnp.where(row >= 1, pltpu.roll(x, 1, 0), 0.0)
        nxt = jnp.where(row <= t_len - 2, pltpu.roll(x, t_len - 1, 0), 0.0)
        y = prev * w_ref[0:1, :] + x * w_ref[1:2, :] + nxt * w_ref[2:3, :] + b_ref[...]
        return y * _sigmoid(y)

    def lane_bcast(x, idx):
        return jnp.broadcast_to(jnp.sum(jnp.where(lane == idx, x, 0.0), axis=-1, keepdims=True), x.shape)

    qs_ref[...] = (conv_silu(q_ref[...], cwq_ref, cbq_ref) * (HEAD_DIM ** -0.5)).astype(BF16)
    k = conv_silu(k_ref[...], cwk_ref, cbk_ref)
    ks_ref[...] = k.astype(BF16)
    for c in range(n_chunks):
        kt_ref[:, c * CHUNK:(c + 1) * CHUNK] = k[c * CHUNK:(c + 1) * CHUNK, :].T
    va_ref[:, :HEAD_DIM] = v_ref[...].astype(BF16)
    va_ref[:, HEAD_DIM:] = jnp.ones((t_len, HEAD_DIM), BF16)
    bf_ref[...] = lane_bcast(bp_ref[...], H_MLSTM + head)
    bb_ref[...] = lane_bcast(bs_ref[...], 3 * H_MLSTM + head)
    sf_ref[...] = jnp.zeros_like(sf_ref)
    sb_ref[...] = jnp.zeros_like(sb_ref)

    ti = lax.broadcasted_iota(jnp.int32, (CHUNK, CHUNK), 0)
    si = lax.broadcasted_iota(jnp.int32, (CHUNK, CHUNK), 1)
    mask_f = ti >= si
    mask_b = ti <= si
    sub = lax.broadcasted_iota(jnp.int32, (2 * H_MLSTM, 1), 0)

    def rowsel(x):
        return jnp.sum(jnp.where(sub == head, x, 0.0), axis=0, keepdims=True)

    mf = mb = jnp.full((1, 1), NEG, F32)
    for i in range(n_chunks):
        sf = slice(i * CHUNK, (i + 1) * CHUNK)
        sb = slice((n_chunks - 1 - i) * CHUNK, (n_chunks - i) * CHUNK)

        b = bf_ref[sf, :]
        hf, mf = _mlstm_step(qs_ref[sf, :], ks_ref[sf, :], kt_ref[:, sf], va_ref[sf, :], b,
                             rowsel(ut_ref[0:2 * H_MLSTM, sf]), mask_f, b[CHUNK - 1:CHUNK, 0:1], sf_ref, mf)
        hf_ref[sf, :] = hf

        b = bb_ref[sb, :]
        hb, mb = _mlstm_step(qs_ref[sb, :], ks_ref[sb, :], kt_ref[:, sb], va_ref[sb, :], b,
                             rowsel(ut_ref[2 * H_MLSTM:4 * H_MLSTM, sb]), mask_b, b[0:1, 0:1], sb_ref, mb)
        hb_ref[sb, :] = hb

    y = _sigmoid(o_ref[...]) * (hf_ref[...] + hb_ref[...])
    out_ref[...] = _rms(y, gain_ref[...]).astype(out_ref.dtype)


def _mlstm(proj, gates, conv_w, conv_b, gate_b, gain, *, batch, t_len):
    w = H_MLSTM * HEAD_DIM
    blk = lambda off: pl.BlockSpec((None, t_len, HEAD_DIM), lambda b, h: (off + h, b, 0))
    vec = lambda off: pl.BlockSpec((1, HEAD_DIM), lambda b, h: (0, off + h))
    seq = pltpu.VMEM((t_len, HEAD_DIM), F32)
    seqb = pltpu.VMEM((t_len, HEAD_DIM), BF16)
    state = pltpu.VMEM((HEAD_DIM, 2 * HEAD_DIM), F32)
    return pl.pallas_call(
        _mlstm_kernel,
        grid=(batch, H_MLSTM),
        in_specs=[
            blk(_BLK_MQ), blk(_BLK_MK), blk(_BLK_MV), blk(_BLK_MO),
            pl.BlockSpec((t_len, LANES), lambda b, h: (b, 0)),
            pl.BlockSpec((3, HEAD_DIM), lambda b, h: (0, h)),
            pl.BlockSpec((3, HEAD_DIM), lambda b, h: (0, H_MLSTM + h)),
            vec(0), vec(H_MLSTM),
            pl.BlockSpec((1, LANES), lambda b, h: (0, 0)),
            vec(0),
        ],
        out_specs=pl.BlockSpec((t_len, HEAD_DIM), lambda b, h: (b, h)),
        out_shape=jax.ShapeDtypeStruct((batch * t_len, w), BF16),
        scratch_shapes=[
            seqb, seqb, pltpu.VMEM((HEAD_DIM, t_len), F32), pltpu.VMEM((t_len, 2 * HEAD_DIM), BF16),
            seq, seq, pltpu.VMEM((4 * H_MLSTM, t_len), F32), seq, seq, seq, seq, state, state,
        ],
        compiler_params=_params("parallel", "arbitrary"),
        name="mlstm",
    )(proj, proj, proj, proj, gates, conv_w, conv_w, conv_b.reshape(1, -1), conv_b.reshape(1, -1),
      gate_b, gain.reshape(1, -1))


def _na_kernel(q_ref, k_ref, v_ref, bias_ref, out_ref, qb_ref, kb_ref, va_ref):
    t_len = q_ref.shape[0]
    rows = t_len // GRID_W
    wr = min(WIN_R, rows)
    n_keys = wr * GRID_W

    qb_ref[...] = (q_ref[...] * (HEAD_DIM ** -0.5)).astype(BF16)
    kb_ref[...] = k_ref[...].astype(BF16)
    va_ref[:, :HEAD_DIM] = v_ref[...].astype(BF16)
    va_ref[:, HEAD_DIM:] = jnp.ones((t_len, HEAD_DIM), BF16)

    for r in range(rows):
        r0 = min(max(r - wr // 2, 0), rows - wr)
        sq = slice(r * GRID_W, (r + 1) * GRID_W)
        sk = slice(r0 * GRID_W, r0 * GRID_W + n_keys)
        s = _dot_nt(qb_ref[sq, :], kb_ref[sk, :]) + bias_ref[0, r - r0]
        e = jnp.exp(s - jnp.max(s, axis=-1, keepdims=True))
        o = _dot(e.astype(BF16), va_ref[sk, :])
        out_ref[sq, :] = (o[:, :HEAD_DIM] / o[:, HEAD_DIM:]).astype(out_ref.dtype)


def _na_bias_table(rpb, rows):
    wr = min(WIN_R, rows)
    off = np.arange(wr)[:, None, None, None]
    qc = np.arange(GRID_W)[None, :, None, None]
    kr = np.arange(wr)[None, None, :, None]
    kc = np.arange(GRID_W)[None, None, None, :]
    win_c0 = np.clip(qc - WIN_C // 2, 0, GRID_W - WIN_C)
    ok = (kc >= win_c0) & (kc < win_c0 + WIN_C)
    sel_r = (np.arange(2 * WIN_R - 1) == (kr - off + WIN_R - 1)[..., None])[:, 0, :, 0, :]
    sel_c = (np.arange(2 * WIN_C - 1) == (kc - qc + WIN_C - 1)[..., None])[0, :, 0, :, :]
    tab = jnp.einsum("okr,hrc,qjc->hoqkj", jnp.asarray(sel_r, F32), rpb.astype(F32), jnp.asarray(sel_c, F32),
                     precision=lax.Precision.HIGHEST)
    tab = jnp.where(jnp.asarray(ok), tab, NEG)
    return tab.reshape(rpb.shape[0], wr, GRID_W, wr * GRID_W)


def _na(proj, bias_tab, *, batch, t_len):
    blk = lambda off: pl.BlockSpec((None, t_len, HEAD_DIM), lambda b, h: (off + h, b, 0))
    seq = pltpu.VMEM((t_len, HEAD_DIM), BF16)
    return pl.pallas_call(
        _na_kernel,
        grid=(batch, H_NA),
        in_specs=[
            blk(_BLK_NQ), blk(_BLK_NK), blk(_BLK_NV),
            pl.BlockSpec((1,) + bias_tab.shape[1:], lambda b, h: (h, 0, 0, 0)),
        ],
        out_specs=pl.BlockSpec((t_len, HEAD_DIM), lambda b, h: (b, h)),
        out_shape=jax.ShapeDtypeStruct((batch * t_len, H_NA * HEAD_DIM), BF16),
        scratch_shapes=[seq, seq, pltpu.VMEM((t_len, 2 * HEAD_DIM), BF16)],
        compiler_params=_params("parallel", "parallel"),
        name="natten",
    )(proj, proj, proj, bias_tab)


def _ret_kernel(q_ref, k_ref, v_ref, g_ref, cos_ref, sin_ref, dl_ref, gain_ref, out_ref,
                qb_ref, qx_ref, kb_ref, kz_ref, vb_ref, o_ref, kv_ref, r_ref):
    t_len = q_ref.shape[0]
    n_chunks = t_len // CHUNK

    def rope(x):
        return x * cos_ref[...] + pltpu.roll(x, HEAD_DIM // 2, 1) * sin_ref[...]

    lg = _log_sigmoid(dl_ref[0])
    lg_f, lg_b = lg[0:1, :], lg[1:2, :]
    diff = (lax.broadcasted_iota(jnp.int32, (CHUNK, CHUNK), 0)
            - lax.broadcasted_iota(jnp.int32, (CHUNK, CHUNK), 1)).astype(F32)
    intra = (jnp.where(diff >= 0, jnp.exp(jnp.where(diff >= 0, diff, 0.0) * lg_f), 0.0)
             + jnp.where(diff <= 0, jnp.exp(jnp.where(diff <= 0, -diff, 0.0) * lg_b), 0.0))
    gch_f = jnp.exp(CHUNK * lg_f)
    gch_b = jnp.exp(CHUNK * lg_b)
    pos = lax.broadcasted_iota(jnp.int32, (CHUNK, HEAD_DIM), 0).astype(F32)

    xi_f = jnp.exp((pos + 1.0) * lg_f)
    xi_b = jnp.exp((CHUNK - pos) * lg_b)
    zeta_f = jnp.exp((CHUNK - 1.0 - pos) * lg_f)
    zeta_b = jnp.exp(pos * lg_b)
    q = rope(q_ref[...])
    k = rope(k_ref[...]) * (HEAD_DIM ** -0.5)
    qb_ref[...] = q.astype(BF16)
    kb_ref[...] = k.astype(BF16)
    vb_ref[...] = v_ref[...].astype(BF16)
    for c in range(n_chunks):
        cs = slice(c * CHUNK, (c + 1) * CHUNK)
        qx_ref[cs, :HEAD_DIM] = (q[cs, :] * xi_f).astype(BF16)
        qx_ref[cs, HEAD_DIM:] = (q[cs, :] * xi_b).astype(BF16)
        kz_ref[:HEAD_DIM, cs] = (k[cs, :] * zeta_f).T.astype(BF16)
        kz_ref[HEAD_DIM:, cs] = (k[cs, :] * zeta_b).T.astype(BF16)

    def chunk(c):
        return slice(c * CHUNK, (c + 1) * CHUNK)

    for c in range(n_chunks):
        sl = chunk(c)
        vb = vb_ref[sl, :]
        s = _dot_nt(qb_ref[sl, :], kb_ref[sl, :]) * intra
        o_ref[sl, :] = _dot(s.astype(BF16), vb)
        kv_ref[c] = _dot(kz_ref[:, sl], vb)

    def state_pass(i, carry):
        rf, rb = carry
        cb = n_chunks - 1 - i
        r_ref[i, :HEAD_DIM, :] = rf.astype(BF16)
        r_ref[cb, HEAD_DIM:, :] = rb.astype(BF16)
        return gch_f * rf + kv_ref[i, :HEAD_DIM, :], gch_b * rb + kv_ref[cb, HEAD_DIM:, :]

    zero = jnp.zeros((HEAD_DIM, HEAD_DIM), F32)
    lax.fori_loop(0, n_chunks, state_pass, (zero, zero))

    for c in range(n_chunks):
        o_ref[chunk(c), :] += _dot(qx_ref[chunk(c), :], r_ref[c])

    g = g_ref[...]
    out_ref[...] = (g * _sigmoid(g) * _rms(o_ref[...], gain_ref[...])).astype(out_ref.dtype)


def _ret(proj, cos2, sin2, decay_rows, gain, *, batch, t_len):
    n_chunks = t_len // CHUNK
    blk = lambda off: pl.BlockSpec((None, t_len, HEAD_DIM), lambda b, h: (off + h, b, 0))
    tab = pl.BlockSpec((t_len, HEAD_DIM), lambda b, h: (0, 0))
    seqb = pltpu.VMEM((t_len, HEAD_DIM), BF16)
    return pl.pallas_call(
        _ret_kernel,
        grid=(batch, H_RET),
        in_specs=[
            blk(_BLK_RQ), blk(_BLK_RK), blk(_BLK_RV), blk(_BLK_RG), tab, tab,
            pl.BlockSpec((1, 2, LANES), lambda b, h: (h, 0, 0)),
            pl.BlockSpec((1, HEAD_DIM), lambda b, h: (0, h)),
        ],
        out_specs=pl.BlockSpec((t_len, HEAD_DIM), lambda b, h: (b, h)),
        out_shape=jax.ShapeDtypeStruct((batch * t_len, H_RET * HEAD_DIM), BF16),
        scratch_shapes=[
            seqb, pltpu.VMEM((t_len, 2 * HEAD_DIM), BF16), seqb, pltpu.VMEM((2 * HEAD_DIM, t_len), BF16), seqb,
            pltpu.VMEM((t_len, HEAD_DIM), F32),
            pltpu.VMEM((n_chunks, 2 * HEAD_DIM, HEAD_DIM), F32),
            pltpu.VMEM((n_chunks, 2 * HEAD_DIM, HEAD_DIM), BF16),
        ],
        compiler_params=_params("parallel", "parallel"),
        name="retention",
    )(proj, proj, proj, proj, cos2, sin2, decay_rows, gain.reshape(1, -1))


def _rope_tables(t_len):
    inv = ROPE_BASE ** (-jnp.arange(0, HEAD_DIM, 2, dtype=F32) / HEAD_DIM)
    ang = jnp.arange(t_len, dtype=F32)[:, None] * inv[None, :]
    cos, sin = jnp.cos(ang), jnp.sin(ang)
    return jnp.concatenate([cos, cos], axis=-1), jnp.concatenate([-sin, sin], axis=-1)


def _pick(n, pref):
    return pref if n % pref == 0 else n


def kernel(x, ffn1_norm, ffn1_w_gu, ffn1_w_down, mix_norm, w_in, mlstm_conv_w, mlstm_conv_b, mlstm_gate_b,
           mlstm_head_norm, na_rpb, ret_decay_logit, ret_head_norm, w_out, ffn2_norm, ffn2_w_gu, ffn2_w_down,
           final_norm):
    batch, t_len, d = x.shape
    n = batch * t_len
    depth = ffn1_norm.shape[0]
    rows = t_len // GRID_W
    n_gates = 4 * H_MLSTM
    gate_lo = 4 * H_MLSTM * HEAD_DIM

    tm = _pick(n, 1024)
    ffn = functools.partial(_ffn, tm=tm, tf=512)
    cos2, sin2 = _rope_tables(t_len)

    h = x.reshape(n, d)
    for l in range(depth):
        h = ffn(h, ffn1_norm[l], _cast_layer(ffn1_w_gu, l, tr=256), _cast_layer(ffn1_w_down, l, tr=704),
                final_norm, final_norm=False)

        w_main, w_gate = _cast_w_in(w_in, l, gate_lo=gate_lo, n_gates=n_gates, tc=512)
        proj, gates = _mix_in(h, mix_norm[l], w_main, w_gate, tm=tm, tn=1792)

        gate_b = jnp.pad(mlstm_gate_b[l].astype(F32).reshape(1, n_gates), ((0, 0), (0, LANES - n_gates)))
        m_out = _mlstm(proj, gates, mlstm_conv_w[l], mlstm_conv_b[l], gate_b, mlstm_head_norm[l],
                       batch=batch, t_len=t_len)
        n_out = _na(proj, _na_bias_table(na_rpb[l], rows), batch=batch, t_len=t_len)
        decay_rows = jnp.broadcast_to(ret_decay_logit[l].astype(F32).T[:, :, None], (H_RET, 2, LANES))
        r_out = _ret(proj, cos2, sin2, decay_rows, ret_head_norm[l], batch=batch, t_len=t_len)

        h = _mix_out(h, m_out, n_out, r_out, _cast_layer(w_out, l, tr=512), tm=_pick(n, 512), tn=d)

        h = ffn(h, ffn2_norm[l], _cast_layer(ffn2_w_gu, l, tr=256), _cast_layer(ffn2_w_down, l, tr=704),
                final_norm, final_norm=(l == depth - 1))
    return h.reshape(batch, t_len, d)
```

```python
import functools

import numpy as np
import jax
import jax.numpy as jnp
from jax import lax
from jax.experimental import pallas as pl
from jax.experimental.pallas import tpu as pltpu

F32 = jnp.float32
BF16 = jnp.bfloat16

HEAD_DIM = 128
H_MLSTM = 4
H_NA = 8
H_RET = 4
CHUNK = 128
GRID_W = 64
WIN_R = 8
WIN_C = 16
ROPE_BASE = 10000.0
EPS = 1e-6
NEG = -1e30

LANES = 128
VMEM_BYTES = 64 * 1024 * 1024
VMEM_LIMIT_BYTES = VMEM_BYTES - 8 * 1024 * 1024
FFN_VMEM_LIMIT_BYTES = VMEM_BYTES - 3 * 1024 * 1024

_BLK_MQ, _BLK_MK, _BLK_MV, _BLK_MO = 0, 4, 8, 12
_BLK_NQ, _BLK_NK, _BLK_NV = 16, 24, 32
_BLK_RQ, _BLK_RK, _BLK_RV, _BLK_RG = 40, 44, 48, 52


def _params(*sem, vmem=VMEM_LIMIT_BYTES):
    return pltpu.CompilerParams(dimension_semantics=sem, vmem_limit_bytes=vmem)


def _sigmoid(x):
    return 1.0 / (1.0 + jnp.exp(-x))


def _log_sigmoid(x):
    return jnp.minimum(x, 0.0) - jnp.log1p(jnp.exp(-jnp.abs(x)))


def _rms(x, g):
    return x * lax.rsqrt(jnp.mean(x * x, axis=-1, keepdims=True) + EPS) * g


def _dot(a, b):
    return jnp.dot(a, b, preferred_element_type=F32)


def _dot_nt(a, b):
    return lax.dot_general(a, b, (((1,), (1,)), ((), ())), preferred_element_type=F32)


def _cast_kernel(w_ref, o_ref):
    o_ref[...] = w_ref[...].astype(o_ref.dtype)


def _cast_layer(w, layer, *, tr):
    _, r, c = w.shape
    return pl.pallas_call(
        _cast_kernel,
        grid=(r // tr,),
        in_specs=[pl.BlockSpec((None, tr, c), lambda i: (layer, i, 0))],
        out_specs=pl.BlockSpec((tr, c), lambda i: (i, 0)),
        out_shape=jax.ShapeDtypeStruct((r, c), BF16),
        compiler_params=_params("parallel"),
        name="cast_weight",
    )(w)


def _cast_w_in_kernel(wt_ref, gt_ref, main_ref, gate_ref, *, n_gates):
    main_ref[...] = wt_ref[0].T.astype(BF16)

    @pl.when(pl.program_id(0) == 0)
    def _():
        gate_ref[...] = jnp.zeros_like(gate_ref)
        gate_ref[:, :n_gates] = gt_ref[0].T.astype(BF16)


def _cast_w_in(w_in, layer, *, gate_lo, n_gates, tc):
    wt = jnp.swapaxes(w_in, 1, 2)
    _, c, d = wt.shape
    n_main = c - n_gates

    def src_rows(i):
        return pl.multiple_of(jnp.where(i * tc >= gate_lo, i * tc + n_gates, i * tc), n_gates)

    return pl.pallas_call(
        functools.partial(_cast_w_in_kernel, n_gates=n_gates),
        grid=(n_main // tc,),
        in_specs=[
            pl.BlockSpec((pl.Element(1), pl.Element(tc), pl.Element(d)), lambda i: (layer, src_rows(i), 0)),
            pl.BlockSpec((pl.Element(1), pl.Element(n_gates), pl.Element(d)), lambda i: (layer, gate_lo, 0)),
        ],
        out_specs=[pl.BlockSpec((d, tc), lambda i: (0, i)), pl.BlockSpec((d, LANES), lambda i: (0, 0))],
        out_shape=[jax.ShapeDtypeStruct((d, n_main), BF16), jax.ShapeDtypeStruct((d, LANES), BF16)],
        compiler_params=_params("arbitrary"),
        name="cast_w_in",
    )(wt, wt)


_FFN_ROWS = 256


def _ffn_kernel(x_ref, g_ref, wg_ref, wu_ref, wd_ref, fg_ref, *rest, final_norm, n_cast):
    cast_in, (o_ref, *cast_out), xn_ref = rest[:n_cast], rest[n_cast:2 * n_cast + 1], rest[2 * n_cast + 1]
    j = pl.program_id(1)
    tm, d = o_ref.shape

    def rows(c):
        return pl.ds(pl.multiple_of(c * _FFN_ROWS, _FFN_ROWS), _FFN_ROWS)

    @pl.when(j == 0)
    def _():
        def norm_rows(c, carry):
            xn_ref[rows(c), :] = _rms(x_ref[rows(c), :], g_ref[...]).astype(BF16)
            return carry
        lax.fori_loop(0, tm // _FFN_ROWS, norm_rows, 0)
        o_ref[...] = jnp.zeros_like(o_ref)

    xn = xn_ref[...]
    gate = _dot(xn, wg_ref[...])
    up = _dot(xn, wu_ref[...])
    act = (gate * _sigmoid(gate) * up).astype(BF16)
    o_ref[...] += _dot(act, wd_ref[...])

    for src, dst in zip(cast_in, cast_out):
        dst[...] = src[...].astype(dst.dtype)

    @pl.when(j == pl.num_programs(1) - 1)
    def _():
        def finish_rows(c, carry):
            h = x_ref[rows(c), :] + 0.5 * o_ref[rows(c), :]
            if final_norm:
                h = _rms(h, fg_ref[...])
            o_ref[rows(c), :] = h
            return carry
        lax.fori_loop(0, tm // _FFN_ROWS, finish_rows, 0)


def _cast_plan(w, n_i, n_j):
    _, r, c = w.shape
    steps = n_i * n_j
    if r % n_i == 0 and c % n_j == 0 and (c // n_j) % LANES == 0 and (r // n_i) % 16 == 0:
        return (r // n_i, c // n_j), (lambda i, j: (i, j))
    assert r % steps == 0 and (r // steps) % 16 == 0, (w.shape, n_i, n_j)
    return (r // steps, c), (lambda i, j: (i * n_j + j, 0))


def _ffn(h, g, w_gu, w_down, final_g, *, final_norm, tm, tf, cast=()):
    n, d = h.shape
    f = w_down.shape[0]
    nf = f // tf
    grid = (n // tm, nf)
    cast_specs_in, cast_specs_out, cast_shapes = [], [], []
    for w, layer in cast:
        blk, imap = _cast_plan(w, *grid)
        cast_specs_in.append(pl.BlockSpec((None,) + blk, functools.partial(lambda i, j, m, l: (l,) + m(i, j), m=imap, l=layer)))
        cast_specs_out.append(pl.BlockSpec(blk, imap))
        cast_shapes.append(jax.ShapeDtypeStruct(w.shape[1:], BF16))
    outs = pl.pallas_call(
        functools.partial(_ffn_kernel, final_norm=final_norm, n_cast=len(cast)),
        grid=grid,
        in_specs=[
            pl.BlockSpec((tm, d), lambda i, j: (i, 0)),
            pl.BlockSpec((1, d), lambda i, j: (0, 0)),
            pl.BlockSpec((d, tf), lambda i, j: (0, j)),
            pl.BlockSpec((d, tf), lambda i, j: (0, j + nf)),
            pl.BlockSpec((tf, d), lambda i, j: (j, 0)),
            pl.BlockSpec((1, d), lambda i, j: (0, 0)),
        ] + cast_specs_in,
        out_specs=[pl.BlockSpec((tm, d), lambda i, j: (i, 0))] + cast_specs_out,
        out_shape=[jax.ShapeDtypeStruct((n, d), F32)] + cast_shapes,
        scratch_shapes=[pltpu.VMEM((tm, d), BF16)],
        compiler_params=_params("parallel", "arbitrary", vmem=FFN_VMEM_LIMIT_BYTES),
        name="ffn",
    )(h, g.reshape(1, d), w_gu, w_gu, w_down, final_g.reshape(1, d), *[w for w, _ in cast])
    return outs[0], outs[1:]


def _mix_in_kernel(x_ref, g_ref, w_ref, wgate_ref, o_ref, og_ref, xn_ref):
    @pl.when(pl.program_id(1) == 0)
    def _():
        xn = _rms(x_ref[...], g_ref[...]).astype(BF16)
        xn_ref[...] = xn
        og_ref[...] = _dot(xn, wgate_ref[...])

    res = _dot(xn_ref[...], w_ref[...])
    for k in range(o_ref.shape[0]):
        o_ref[k] = res[:, k * HEAD_DIM:(k + 1) * HEAD_DIM]


def _mix_in(h, g, w_main, w_gate, *, tm, tn):
    n, d = h.shape
    c = w_main.shape[1]
    nb = tn // HEAD_DIM
    return pl.pallas_call(
        _mix_in_kernel,
        grid=(n // tm, c // tn),
        in_specs=[
            pl.BlockSpec((tm, d), lambda i, j: (i, 0)),
            pl.BlockSpec((1, d), lambda i, j: (0, 0)),
            pl.BlockSpec((d, tn), lambda i, j: (0, j)),
            pl.BlockSpec((d, LANES), lambda i, j: (0, 0)),
        ],
        out_specs=[
            pl.BlockSpec((nb, tm, HEAD_DIM), lambda i, j: (j, i, 0)),
            pl.BlockSpec((tm, LANES), lambda i, j: (i, 0)),
        ],
        out_shape=[jax.ShapeDtypeStruct((c // HEAD_DIM, n, HEAD_DIM), F32), jax.ShapeDtypeStruct((n, LANES), F32)],
        scratch_shapes=[pltpu.VMEM((tm, d), BF16)],
        compiler_params=_params("parallel", "arbitrary"),
        name="mix_in",
    )(h, g.reshape(1, d), w_main, w_gate)


def _mix_out_kernel(h_ref, m_ref, na_ref, nb_ref, r_ref, wm_ref, wna_ref, wnb_ref, wr_ref, o_ref):
    acc = (_dot(m_ref[...], wm_ref[...]) + _dot(na_ref[...], wna_ref[...])
           + _dot(nb_ref[...], wnb_ref[...]) + _dot(r_ref[...], wr_ref[...]))
    o_ref[...] = h_ref[...] + acc


def _mix_out(h, m_out, n_out, r_out, w_out, *, tm, tn):
    n, d = h.shape
    kb = m_out.shape[1]
    assert n_out.shape[1] == 2 * kb and r_out.shape[1] == kb and w_out.shape[0] == 4 * kb
    act = lambda c: pl.BlockSpec((tm, kb), lambda i, j: (i, c))
    wblk = lambda r: pl.BlockSpec((kb, tn), lambda i, j: (r, j))
    return pl.pallas_call(
        _mix_out_kernel,
        grid=(n // tm, d // tn),
        in_specs=[pl.BlockSpec((tm, tn), lambda i, j: (i, j)), act(0), act(0), act(1), act(0),
                  wblk(0), wblk(1), wblk(2), wblk(3)],
        out_specs=pl.BlockSpec((tm, tn), lambda i, j: (i, j)),
        out_shape=jax.ShapeDtypeStruct((n, d), F32),
        compiler_params=_params("parallel", "arbitrary"),
        name="mix_out",
    )(h, m_out, n_out, n_out, r_out, w_out, w_out, w_out, w_out)


def _mlstm_step(qb, kb, kt, va, b, u_row, mask, g, st_ref, m_st):
    dmat = jnp.where(mask, b + u_row, NEG)
    inter = b + m_st
    m_t = jnp.maximum(inter, jnp.max(dmat, axis=-1, keepdims=True))
    s = _dot_nt(qb, kb) * jnp.exp(dmat - m_t)
    st = st_ref[...]
    r_intra = _dot(s.astype(BF16), va)
    r_inter = _dot(qb, st.astype(BF16))
    w_inter = jnp.exp(inter - m_t)
    num = r_intra[:, :HEAD_DIM] + w_inter * r_inter[:, :HEAD_DIM]
    den = r_intra[:, HEAD_DIM:] + w_inter * r_inter[:, HEAD_DIM:]
    h_out = num / jnp.maximum(jnp.abs(den), jnp.exp(-m_t))
    a = g + u_row
    m_new = jnp.maximum(g + m_st, jnp.max(a, axis=-1, keepdims=True))
    ktw = (kt * jnp.exp(a - m_new)).astype(BF16)
    st_ref[...] = jnp.exp(g + m_st - m_new) * st + _dot(ktw, va)
    return h_out, m_new


def _mlstm_kernel(q_ref, k_ref, v_ref, o_ref, gt_ref, cwq_ref, cwk_ref, cbq_ref, cbk_ref, gb_ref, gain_ref,
                  out_ref, qs_ref, ks_ref, kt_ref, va_ref, bp_ref, bs_ref, ut_ref, bf_ref, bb_ref, hf_ref, hb_ref,
                  sf_ref, sb_ref):
    t_len = q_ref.shape[0]
    n_chunks = t_len // CHUNK
    head = pl.program_id(1)
    row = lax.broadcasted_iota(jnp.int32, (t_len, 1), 0)
    lane = lax.broadcasted_iota(jnp.int32, (1, LANES), 1)

    @pl.when(head == 0)
    def _():
        gates = gt_ref[...] + gb_ref[...]
        logf = _log_sigmoid(gates)
        rin = row & (CHUNK - 1)
        bp = logf
        bs = logf
        k = 1
        while k < CHUNK:
            bp = bp + jnp.where(rin >= k, pltpu.roll(bp, k, 0), 0.0)
            bs = bs + jnp.where(rin < CHUNK - k, pltpu.roll(bs, t_len - k, 0), 0.0)
            k *= 2
        bp_ref[...] = bp
        bs_ref[...] = bs
        u = gates - jnp.where(lane < 2 * H_MLSTM, pltpu.roll(bp, LANES - H_MLSTM, 1),
                              pltpu.roll(bs, LANES - H_MLSTM, 1))
        for c in range(n_chunks):
            ut_ref[:, c * CHUNK:(c + 1) * CHUNK] = u[c * CHUNK:(c + 1) * CHUNK, :].T[0:ut_ref.shape[0], :]

    def conv_silu(x, w_ref, b_ref):
        prev = jnp.where(row >= 1, pltpu.roll(x, 1, 0), 0.0)
        nxt = jnp.where(row <= t_len - 2, pltpu.roll(x, t_len - 1, 0), 0.0)
        y = prev * w_ref[0:1, :] + x * w_ref[1:2, :] + nxt * w_ref[2:3, :] + b_ref[...]
        return y * _sigmoid(y)

    def lane_bcast(x, idx):
        return jnp.broadcast_to(jnp.sum(jnp.where(lane == idx, x, 0.0), axis=-1, keepdims=True), x.shape)

    qs_ref[...] = (conv_silu(q_ref[...], cwq_ref, cbq_ref) * (HEAD_DIM ** -0.5)).astype(BF16)
    k = conv_silu(k_ref[...], cwk_ref, cbk_ref)
    ks_ref[...] = k.astype(BF16)
    for c in range(n_chunks):
        kt_ref[:, c * CHUNK:(c + 1) * CHUNK] = k[c * CHUNK:(c + 1) * CHUNK, :].T
    va_ref[:, :HEAD_DIM] = v_ref[...].astype(BF16)
    va_ref[:, HEAD_DIM:] = jnp.ones((t_len, HEAD_DIM), BF16)
    bf_ref[...] = lane_bcast(bp_ref[...], H_MLSTM + head)
    bb_ref[...] = lane_bcast(bs_ref[...], 3 * H_MLSTM + head)
    sf_ref[...] = jnp.zeros_like(sf_ref)
    sb_ref[...] = jnp.zeros_like(sb_ref)

    ti = lax.broadcasted_iota(jnp.int32, (CHUNK, CHUNK), 0)
    si = lax.broadcasted_iota(jnp.int32, (CHUNK, CHUNK), 1)
    mask_f = ti >= si
    mask_b = ti <= si
    sub = lax.broadcasted_iota(jnp.int32, (2 * H_MLSTM, 1), 0)

    def rowsel(x):
        return jnp.sum(jnp.where(sub == head, x, 0.0), axis=0, keepdims=True)

    mf = mb = jnp.full((1, 1), NEG, F32)
    for i in range(n_chunks):
        sf = slice(i * CHUNK, (i + 1) * CHUNK)
        sb = slice((n_chunks - 1 - i) * CHUNK, (n_chunks - i) * CHUNK)

        b = bf_ref[sf, :]
        hf, mf = _mlstm_step(qs_ref[sf, :], ks_ref[sf, :], kt_ref[:, sf], va_ref[sf, :], b,
                             rowsel(ut_ref[0:2 * H_MLSTM, sf]), mask_f, b[CHUNK - 1:CHUNK, 0:1], sf_ref, mf)
        hf_ref[sf, :] = hf

        b = bb_ref[sb, :]
        hb, mb = _mlstm_step(qs_ref[sb, :], ks_ref[sb, :], kt_ref[:, sb], va_ref[sb, :], b,
                             rowsel(ut_ref[2 * H_MLSTM:4 * H_MLSTM, sb]), mask_b, b[0:1, 0:1], sb_ref, mb)
        hb_ref[sb, :] = hb

    y = _sigmoid(o_ref[...]) * (hf_ref[...] + hb_ref[...])
    out_ref[...] = _rms(y, gain_ref[...]).astype(out_ref.dtype)


def _mlstm(proj, gates, conv_w, conv_b, gate_b, gain, *, batch, t_len):
    w = H_MLSTM * HEAD_DIM
    blk = lambda off: pl.BlockSpec((None, t_len, HEAD_DIM), lambda b, h: (off + h, b, 0))
    vec = lambda off: pl.BlockSpec((1, HEAD_DIM), lambda b, h: (0, off + h))
    seq = pltpu.VMEM((t_len, HEAD_DIM), F32)
    seqb = pltpu.VMEM((t_len, HEAD_DIM), BF16)
    state = pltpu.VMEM((HEAD_DIM, 2 * HEAD_DIM), F32)
    return pl.pallas_call(
        _mlstm_kernel,
        grid=(batch, H_MLSTM),
        in_specs=[
            blk(_BLK_MQ), blk(_BLK_MK), blk(_BLK_MV), blk(_BLK_MO),
            pl.BlockSpec((t_len, LANES), lambda b, h: (b, 0)),
            pl.BlockSpec((3, HEAD_DIM), lambda b, h: (0, h)),
            pl.BlockSpec((3, HEAD_DIM), lambda b, h: (0, H_MLSTM + h)),
            vec(0), vec(H_MLSTM),
            pl.BlockSpec((1, LANES), lambda b, h: (0, 0)),
            vec(0),
        ],
        out_specs=pl.BlockSpec((t_len, HEAD_DIM), lambda b, h: (b, h)),
        out_shape=jax.ShapeDtypeStruct((batch * t_len, w), BF16),
        scratch_shapes=[
            seqb, seqb, pltpu.VMEM((HEAD_DIM, t_len), F32), pltpu.VMEM((t_len, 2 * HEAD_DIM), BF16),
            seq, seq, pltpu.VMEM((4 * H_MLSTM, t_len), F32), seq, seq, seq, seq, state, state,
        ],
        compiler_params=_params("parallel", "arbitrary"),
        name="mlstm",
    )(proj, proj, proj, proj, gates, conv_w, conv_w, conv_b.reshape(1, -1), conv_b.reshape(1, -1),
      gate_b, gain.reshape(1, -1))


_NA_HEADS = 2


def _na_kernel(q_ref, k_ref, v_ref, bias_ref, out_ref, qb_ref, kb_ref, va_ref):
    t_len = q_ref.shape[1]
    rows = t_len // GRID_W
    wr = min(WIN_R, rows)
    n_keys = wr * GRID_W

    for hh in range(_NA_HEADS):
        qb_ref[hh] = (q_ref[hh] * (HEAD_DIM ** -0.5)).astype(BF16)
        kb_ref[hh] = k_ref[hh].astype(BF16)
        va_ref[hh, :, :HEAD_DIM] = v_ref[hh].astype(BF16)
        va_ref[hh, :, HEAD_DIM:] = jnp.ones((t_len, HEAD_DIM), BF16)

    for r in range(rows):
        r0 = min(max(r - wr // 2, 0), rows - wr)
        sq = slice(r * GRID_W, (r + 1) * GRID_W)
        sk = slice(r0 * GRID_W, r0 * GRID_W + n_keys)
        for hh in range(_NA_HEADS):
            s = _dot_nt(qb_ref[hh, sq, :], kb_ref[hh, sk, :]) + bias_ref[hh, r - r0]
            e = jnp.exp(s - jnp.max(s, axis=-1, keepdims=True))
            o = _dot(e.astype(BF16), va_ref[hh, sk, :])
            out_ref[sq, hh * HEAD_DIM:(hh + 1) * HEAD_DIM] = (o[:, :HEAD_DIM] / o[:, HEAD_DIM:]).astype(out_ref.dtype)


def _na_bias_table(rpb, rows):
    wr = min(WIN_R, rows)
    off = np.arange(wr)[:, None, None, None]
    qc = np.arange(GRID_W)[None, :, None, None]
    kr = np.arange(wr)[None, None, :, None]
    kc = np.arange(GRID_W)[None, None, None, :]
    win_c0 = np.clip(qc - WIN_C // 2, 0, GRID_W - WIN_C)
    ok = (kc >= win_c0) & (kc < win_c0 + WIN_C)
    sel_r = (np.arange(2 * WIN_R - 1) == (kr - off + WIN_R - 1)[..., None])[:, 0, :, 0, :]
    sel_c = (np.arange(2 * WIN_C - 1) == (kc - qc + WIN_C - 1)[..., None])[0, :, 0, :, :]
    tab = jnp.einsum("okr,hrc,qjc->hoqkj", jnp.asarray(sel_r, F32), rpb.astype(F32), jnp.asarray(sel_c, F32),
                     precision=lax.Precision.HIGHEST)
    tab = jnp.where(jnp.asarray(ok), tab, NEG)
    return tab.reshape(rpb.shape[0], wr, GRID_W, wr * GRID_W)


def _na(proj, bias_tab, *, batch, t_len):
    assert H_NA % _NA_HEADS == 0 and _BLK_NQ % _NA_HEADS == 0 and _BLK_NK % _NA_HEADS == 0 and _BLK_NV % _NA_HEADS == 0
    blk = lambda off: pl.BlockSpec((_NA_HEADS, t_len, HEAD_DIM), lambda b, h: (off // _NA_HEADS + h, b, 0))
    seq = pltpu.VMEM((_NA_HEADS, t_len, HEAD_DIM), BF16)
    return pl.pallas_call(
        _na_kernel,
        grid=(batch, H_NA // _NA_HEADS),
        in_specs=[
            blk(_BLK_NQ), blk(_BLK_NK), blk(_BLK_NV),
            pl.BlockSpec((_NA_HEADS,) + bias_tab.shape[1:], lambda b, h: (h, 0, 0, 0)),
        ],
        out_specs=pl.BlockSpec((t_len, _NA_HEADS * HEAD_DIM), lambda b, h: (b, h)),
        out_shape=jax.ShapeDtypeStruct((batch * t_len, H_NA * HEAD_DIM), BF16),
        scratch_shapes=[seq, seq, pltpu.VMEM((_NA_HEADS, t_len, 2 * HEAD_DIM), BF16)],
        compiler_params=_params("parallel", "parallel"),
        name="natten",
    )(proj, proj, proj, bias_tab)


def _ret_kernel(q_ref, k_ref, v_ref, g_ref, cos_ref, sin_ref, dl_ref, gain_ref, out_ref,
                qb_ref, qx_ref, kb_ref, kz_ref, vb_ref, o_ref, kv_ref, r_ref):
    t_len = q_ref.shape[0]
    n_chunks = t_len // CHUNK

    def rope(x):
        return x * cos_ref[...] + pltpu.roll(x, HEAD_DIM // 2, 1) * sin_ref[...]

    lg = _log_sigmoid(dl_ref[0])
    lg_f, lg_b = lg[0:1, :], lg[1:2, :]
    diff = (lax.broadcasted_iota(jnp.int32, (CHUNK, CHUNK), 0)
            - lax.broadcasted_iota(jnp.int32, (CHUNK, CHUNK), 1)).astype(F32)
    intra = (jnp.where(diff >= 0, jnp.exp(jnp.where(diff >= 0, diff, 0.0) * lg_f), 0.0)
             + jnp.where(diff <= 0, jnp.exp(jnp.where(diff <= 0, -diff, 0.0) * lg_b), 0.0))
    gch_f = jnp.exp(CHUNK * lg_f)
    gch_b = jnp.exp(CHUNK * lg_b)
    pos = lax.broadcasted_iota(jnp.int32, (CHUNK, HEAD_DIM), 0).astype(F32)

    xi_f = jnp.exp((pos + 1.0) * lg_f)
    xi_b = jnp.exp((CHUNK - pos) * lg_b)
    zeta_f = jnp.exp((CHUNK - 1.0 - pos) * lg_f)
    zeta_b = jnp.exp(pos * lg_b)
    q = rope(q_ref[...])
    k = rope(k_ref[...]) * (HEAD_DIM ** -0.5)
    qb_ref[...] = q.astype(BF16)
    kb_ref[...] = k.astype(BF16)
    vb_ref[...] = v_ref[...].astype(BF16)
    for c in range(n_chunks):
        cs = slice(c * CHUNK, (c + 1) * CHUNK)
        qx_ref[cs, :HEAD_DIM] = (q[cs, :] * xi_f).astype(BF16)
        qx_ref[cs, HEAD_DIM:] = (q[cs, :] * xi_b).astype(BF16)
        kz_ref[:HEAD_DIM, cs] = (k[cs, :] * zeta_f).T.astype(BF16)
        kz_ref[HEAD_DIM:, cs] = (k[cs, :] * zeta_b).T.astype(BF16)

    def chunk(c):
        return slice(c * CHUNK, (c + 1) * CHUNK)

    for c in range(n_chunks):
        sl = chunk(c)
        vb = vb_ref[sl, :]
        s = _dot_nt(qb_ref[sl, :], kb_ref[sl, :]) * intra
        o_ref[sl, :] = _dot(s.astype(BF16), vb)
        kv_ref[c] = _dot(kz_ref[:, sl], vb)

    def state_pass(i, carry):
        rf, rb = carry
        cb = n_chunks - 1 - i
        r_ref[i, :HEAD_DIM, :] = rf.astype(BF16)
        r_ref[cb, HEAD_DIM:, :] = rb.astype(BF16)
        return gch_f * rf + kv_ref[i, :HEAD_DIM, :], gch_b * rb + kv_ref[cb, HEAD_DIM:, :]

    zero = jnp.zeros((HEAD_DIM, HEAD_DIM), F32)
    lax.fori_loop(0, n_chunks, state_pass, (zero, zero))

    for c in range(n_chunks):
        o_ref[chunk(c), :] += _dot(qx_ref[chunk(c), :], r_ref[c])

    g = g_ref[...]
    out_ref[...] = (g * _sigmoid(g) * _rms(o_ref[...], gain_ref[...])).astype(out_ref.dtype)


def _ret(proj, cos2, sin2, decay_rows, gain, *, batch, t_len):
    n_chunks = t_len // CHUNK
    blk = lambda off: pl.BlockSpec((None, t_len, HEAD_DIM), lambda b, h: (off + h, b, 0))
    tab = pl.BlockSpec((t_len, HEAD_DIM), lambda b, h: (0, 0))
    seqb = pltpu.VMEM((t_len, HEAD_DIM), BF16)
    return pl.pallas_call(
        _ret_kernel,
        grid=(batch, H_RET),
        in_specs=[
            blk(_BLK_RQ), blk(_BLK_RK), blk(_BLK_RV), blk(_BLK_RG), tab, tab,
            pl.BlockSpec((1, 2, LANES), lambda b, h: (h, 0, 0)),
            pl.BlockSpec((1, HEAD_DIM), lambda b, h: (0, h)),
        ],
        out_specs=pl.BlockSpec((t_len, HEAD_DIM), lambda b, h: (b, h)),
        out_shape=jax.ShapeDtypeStruct((batch * t_len, H_RET * HEAD_DIM), BF16),
        scratch_shapes=[
            seqb, pltpu.VMEM((t_len, 2 * HEAD_DIM), BF16), seqb, pltpu.VMEM((2 * HEAD_DIM, t_len), BF16), seqb,
            pltpu.VMEM((t_len, HEAD_DIM), F32),
            pltpu.VMEM((n_chunks, 2 * HEAD_DIM, HEAD_DIM), F32),
            pltpu.VMEM((n_chunks, 2 * HEAD_DIM, HEAD_DIM), BF16),
        ],
        compiler_params=_params("parallel", "parallel"),
        name="retention",
    )(proj, proj, proj, proj, cos2, sin2, decay_rows, gain.reshape(1, -1))


def _rope_tables(t_len):
    inv = ROPE_BASE ** (-jnp.arange(0, HEAD_DIM, 2, dtype=F32) / HEAD_DIM)
    ang = jnp.arange(t_len, dtype=F32)[:, None] * inv[None, :]
    cos, sin = jnp.cos(ang), jnp.sin(ang)
    return jnp.concatenate([cos, cos], axis=-1), jnp.concatenate([-sin, sin], axis=-1)


def _pick(n, pref):
    return pref if n % pref == 0 else n


def kernel(x, ffn1_norm, ffn1_w_gu, ffn1_w_down, mix_norm, w_in, mlstm_conv_w, mlstm_conv_b, mlstm_gate_b,
           mlstm_head_norm, na_rpb, ret_decay_logit, ret_head_norm, w_out, ffn2_norm, ffn2_w_gu, ffn2_w_down,
           final_norm):
    batch, t_len, d = x.shape
    n = batch * t_len
    depth = ffn1_norm.shape[0]
    rows = t_len // GRID_W
    n_gates = 4 * H_MLSTM
    gate_lo = 4 * H_MLSTM * HEAD_DIM

    tm = _pick(n, 1024)
    ffn = functools.partial(_ffn, tm=tm, tf=512)
    cos2, sin2 = _rope_tables(t_len)

    h = x.reshape(n, d)
    w_gu1, w_down1 = _cast_layer(ffn1_w_gu, 0, tr=256), _cast_layer(ffn1_w_down, 0, tr=704)
    for l in range(depth):
        h, (w_gu2, w_down2) = ffn(h, ffn1_norm[l], w_gu1, w_down1, final_norm, final_norm=False,
                                  cast=((ffn2_w_gu, l), (ffn2_w_down, l)))

        w_main, w_gate = _cast_w_in(w_in, l, gate_lo=gate_lo, n_gates=n_gates, tc=512)
        proj, gates = _mix_in(h, mix_norm[l], w_main, w_gate, tm=tm, tn=1792)

        gate_b = jnp.pad(mlstm_gate_b[l].astype(F32).reshape(1, n_gates), ((0, 0), (0, LANES - n_gates)))
        m_out = _mlstm(proj, gates, mlstm_conv_w[l], mlstm_conv_b[l], gate_b, mlstm_head_norm[l],
                       batch=batch, t_len=t_len)
        n_out = _na(proj, _na_bias_table(na_rpb[l], rows), batch=batch, t_len=t_len)
        decay_rows = jnp.broadcast_to(ret_decay_logit[l].astype(F32).T[:, :, None], (H_RET, 2, LANES))
        r_out = _ret(proj, cos2, sin2, decay_rows, ret_head_norm[l], batch=batch, t_len=t_len)

        h = _mix_out(h, m_out, n_out, r_out, _cast_layer(w_out, l, tr=512), tm=_pick(n, 512), tn=d)

        last = l == depth - 1
        h, nxt = ffn(h, ffn2_norm[l], w_gu2, w_down2, final_norm, final_norm=last,
                     cast=() if last else ((ffn1_w_gu, l + 1), (ffn1_w_down, l + 1)))
        if not last:
            w_gu1, w_down1 = nxt
    return h.reshape(batch, t_len, d)
```

```python
import functools

import numpy as np
import jax
import jax.numpy as jnp
from jax import lax
from jax.experimental import pallas as pl
from jax.experimental.pallas import tpu as pltpu

F32 = jnp.float32
BF16 = jnp.bfloat16

HEAD_DIM = 128
H_MLSTM = 4
H_NA = 8
H_RET = 4
CHUNK = 128
GRID_W = 64
WIN_R = 8
WIN_C = 16
ROPE_BASE = 10000.0
EPS = 1e-6
NEG = -1e30

LANES = 128
VMEM_BYTES = 64 * 1024 * 1024
VMEM_LIMIT_BYTES = VMEM_BYTES - 8 * 1024 * 1024
FFN_VMEM_LIMIT_BYTES = VMEM_BYTES - 3 * 1024 * 1024

_BLK_MQ, _BLK_MK, _BLK_MV, _BLK_MO = 0, 4, 8, 12
_BLK_NQ, _BLK_NK, _BLK_NV = 16, 24, 32
_BLK_RQ, _BLK_RK, _BLK_RV, _BLK_RG = 40, 44, 48, 52


def _params(*sem, vmem=VMEM_LIMIT_BYTES):
    return pltpu.CompilerParams(dimension_semantics=sem, vmem_limit_bytes=vmem)


def _sigmoid(x):
    return 1.0 / (1.0 + jnp.exp(-x))


def _log_sigmoid(x):
    return jnp.minimum(x, 0.0) - jnp.log1p(jnp.exp(-jnp.abs(x)))


def _rms(x, g):
    return x * lax.rsqrt(jnp.mean(x * x, axis=-1, keepdims=True) + EPS) * g


def _dot(a, b):
    return jnp.dot(a, b, preferred_element_type=F32)


def _dot_nt(a, b):
    return lax.dot_general(a, b, (((1,), (1,)), ((), ())), preferred_element_type=F32)


def _cast_kernel(w_ref, o_ref):
    o_ref[...] = w_ref[...].astype(o_ref.dtype)


def _cast_layer(w, layer, *, tr):
    _, r, c = w.shape
    return pl.pallas_call(
        _cast_kernel,
        grid=(r // tr,),
        in_specs=[pl.BlockSpec((None, tr, c), lambda i: (layer, i, 0))],
        out_specs=pl.BlockSpec((tr, c), lambda i: (i, 0)),
        out_shape=jax.ShapeDtypeStruct((r, c), BF16),
        compiler_params=_params("parallel"),
        name="cast_weight",
    )(w)


def _cast_w_in_kernel(wt_ref, gt_ref, main_ref, gate_ref, *, n_gates):
    main_ref[...] = wt_ref[0].T.astype(BF16)

    @pl.when(pl.program_id(0) == 0)
    def _():
        gate_ref[...] = jnp.zeros_like(gate_ref)
        gate_ref[:, :n_gates] = gt_ref[0].T.astype(BF16)


def _cast_w_in(w_in, layer, *, gate_lo, n_gates, tc):
    wt = jnp.swapaxes(w_in, 1, 2)
    _, c, d = wt.shape
    n_main = c - n_gates

    def src_rows(i):
        return pl.multiple_of(jnp.where(i * tc >= gate_lo, i * tc + n_gates, i * tc), n_gates)

    return pl.pallas_call(
        functools.partial(_cast_w_in_kernel, n_gates=n_gates),
        grid=(n_main // tc,),
        in_specs=[
            pl.BlockSpec((pl.Element(1), pl.Element(tc), pl.Element(d)), lambda i: (layer, src_rows(i), 0)),
            pl.BlockSpec((pl.Element(1), pl.Element(n_gates), pl.Element(d)), lambda i: (layer, gate_lo, 0)),
        ],
        out_specs=[pl.BlockSpec((d, tc), lambda i: (0, i)), pl.BlockSpec((d, LANES), lambda i: (0, 0))],
        out_shape=[jax.ShapeDtypeStruct((d, n_main), BF16), jax.ShapeDtypeStruct((d, LANES), BF16)],
        compiler_params=_params("arbitrary"),
        name="cast_w_in",
    )(wt, wt)


_FFN_ROWS = 256


def _ffn_kernel(x_ref, g_ref, wg_ref, wu_ref, wd_ref, fg_ref, *rest, final_norm, n_cast):
    cast_in, (o_ref, *cast_out), xn_ref = rest[:n_cast], rest[n_cast:2 * n_cast + 1], rest[2 * n_cast + 1]
    j = pl.program_id(1)
    tm, d = o_ref.shape

    def rows(c):
        return pl.ds(pl.multiple_of(c * _FFN_ROWS, _FFN_ROWS), _FFN_ROWS)

    @pl.when(j == 0)
    def _():
        def norm_rows(c, carry):
            xn_ref[rows(c), :] = _rms(x_ref[rows(c), :], g_ref[...]).astype(BF16)
            return carry
        lax.fori_loop(0, tm // _FFN_ROWS, norm_rows, 0)
        o_ref[...] = jnp.zeros_like(o_ref)

    xn = xn_ref[...]
    gate = _dot(xn, wg_ref[...])
    up = _dot(xn, wu_ref[...])
    act = (gate * _sigmoid(gate) * up).astype(BF16)
    o_ref[...] += _dot(act, wd_ref[...])

    for src, dst in zip(cast_in, cast_out):
        dst[...] = src[...].astype(dst.dtype)

    @pl.when(j == pl.num_programs(1) - 1)
    def _():
        def finish_rows(c, carry):
            h = x_ref[rows(c), :] + 0.5 * o_ref[rows(c), :]
            if final_norm:
                h = _rms(h, fg_ref[...])
            o_ref[rows(c), :] = h
            return carry
        lax.fori_loop(0, tm // _FFN_ROWS, finish_rows, 0)


def _cast_plan(w, n_i, n_j):
    _, r, c = w.shape
    steps = n_i * n_j
    if r % n_i == 0 and c % n_j == 0 and (c // n_j) % LANES == 0 and (r // n_i) % 16 == 0:
        return (r // n_i, c // n_j), (lambda i, j: (i, j))
    assert r % steps == 0 and (r // steps) % 16 == 0, (w.shape, n_i, n_j)
    return (r // steps, c), (lambda i, j: (i * n_j + j, 0))


def _ffn(h, g, w_gu, w_down, final_g, *, final_norm, tm, tf, cast=()):
    n, d = h.shape
    f = w_down.shape[0]
    nf = f // tf
    grid = (n // tm, nf)
    cast_specs_in, cast_specs_out, cast_shapes = [], [], []
    for w, layer in cast:
        blk, imap = _cast_plan(w, *grid)
        cast_specs_in.append(pl.BlockSpec((None,) + blk, functools.partial(lambda i, j, m, l: (l,) + m(i, j), m=imap, l=layer)))
        cast_specs_out.append(pl.BlockSpec(blk, imap))
        cast_shapes.append(jax.ShapeDtypeStruct(w.shape[1:], BF16))
    outs = pl.pallas_call(
        functools.partial(_ffn_kernel, final_norm=final_norm, n_cast=len(cast)),
        grid=grid,
        in_specs=[
            pl.BlockSpec((tm, d), lambda i, j: (i, 0)),
            pl.BlockSpec((1, d), lambda i, j: (0, 0)),
            pl.BlockSpec((d, tf), lambda i, j: (0, j)),
            pl.BlockSpec((d, tf), lambda i, j: (0, j + nf)),
            pl.BlockSpec((tf, d), lambda i, j: (j, 0)),
            pl.BlockSpec((1, d), lambda i, j: (0, 0)),
        ] + cast_specs_in,
        out_specs=[pl.BlockSpec((tm, d), lambda i, j: (i, 0))] + cast_specs_out,
        out_shape=[jax.ShapeDtypeStruct((n, d), F32)] + cast_shapes,
        scratch_shapes=[pltpu.VMEM((tm, d), BF16)],
        compiler_params=_params("parallel", "arbitrary", vmem=FFN_VMEM_LIMIT_BYTES),
        name="ffn",
    )(h, g.reshape(1, d), w_gu, w_gu, w_down, final_g.reshape(1, d), *[w for w, _ in cast])
    return outs[0], outs[1:]


def _mix_in_kernel(x_ref, g_ref, w_ref, wgate_ref, wo_ref, o_ref, og_ref, wob_ref, xn_ref):
    @pl.when(pl.program_id(1) == 0)
    def _():
        xn = _rms(x_ref[...], g_ref[...]).astype(BF16)
        xn_ref[...] = xn
        og_ref[...] = _dot(xn, wgate_ref[...])

    res = _dot(xn_ref[...], w_ref[...])
    for k in range(o_ref.shape[0]):
        o_ref[k] = res[:, k * HEAD_DIM:(k + 1) * HEAD_DIM]
    wob_ref[...] = wo_ref[...].astype(BF16)


def _mix_in(h, g, w_main, w_gate, w_out, layer, *, tm, tn):
    n, d = h.shape
    c = w_main.shape[1]
    nb = tn // HEAD_DIM
    grid = (n // tm, c // tn)
    wo_blk, wo_map = _cast_plan(w_out, *grid)
    return pl.pallas_call(
        _mix_in_kernel,
        grid=grid,
        in_specs=[
            pl.BlockSpec((tm, d), lambda i, j: (i, 0)),
            pl.BlockSpec((1, d), lambda i, j: (0, 0)),
            pl.BlockSpec((d, tn), lambda i, j: (0, j)),
            pl.BlockSpec((d, LANES), lambda i, j: (0, 0)),
            pl.BlockSpec((None,) + wo_blk, lambda i, j: (layer,) + wo_map(i, j)),
        ],
        out_specs=[
            pl.BlockSpec((nb, tm, HEAD_DIM), lambda i, j: (j, i, 0)),
            pl.BlockSpec((tm, LANES), lambda i, j: (i, 0)),
            pl.BlockSpec(wo_blk, wo_map),
        ],
        out_shape=[jax.ShapeDtypeStruct((c // HEAD_DIM, n, HEAD_DIM), F32), jax.ShapeDtypeStruct((n, LANES), F32),
                   jax.ShapeDtypeStruct(w_out.shape[1:], BF16)],
        scratch_shapes=[pltpu.VMEM((tm, d), BF16)],
        compiler_params=_params("parallel", "arbitrary"),
        name="mix_in",
    )(h, g.reshape(1, d), w_main, w_gate, w_out)


def _mix_out_kernel(h_ref, m_ref, na_ref, nb_ref, r_ref, wm_ref, wna_ref, wnb_ref, wr_ref, o_ref):
    acc = (_dot(m_ref[...], wm_ref[...]) + _dot(na_ref[...], wna_ref[...])
           + _dot(nb_ref[...], wnb_ref[...]) + _dot(r_ref[...], wr_ref[...]))
    o_ref[...] = h_ref[...] + acc


def _mix_out(h, m_out, n_out, r_out, w_out, *, tm, tn):
    n, d = h.shape
    kb = m_out.shape[1]
    assert n_out.shape[1] == 2 * kb and r_out.shape[1] == kb and w_out.shape[0] == 4 * kb
    act = lambda c: pl.BlockSpec((tm, kb), lambda i, j: (i, c))
    wblk = lambda r: pl.BlockSpec((kb, tn), lambda i, j: (r, j))
    return pl.pallas_call(
        _mix_out_kernel,
        grid=(n // tm, d // tn),
        in_specs=[pl.BlockSpec((tm, tn), lambda i, j: (i, j)), act(0), act(0), act(1), act(0),
                  wblk(0), wblk(1), wblk(2), wblk(3)],
        out_specs=pl.BlockSpec((tm, tn), lambda i, j: (i, j)),
        out_shape=jax.ShapeDtypeStruct((n, d), F32),
        compiler_params=_params("parallel", "arbitrary"),
        name="mix_out",
    )(h, m_out, n_out, n_out, r_out, w_out, w_out, w_out, w_out)


def _mlstm_step(qb, kb, kt, va, b, u_row, mask, g, st_ref, m_st):
    dmat = jnp.where(mask, b + u_row, NEG)
    inter = b + m_st
    m_t = jnp.maximum(inter, jnp.max(dmat, axis=-1, keepdims=True))
    s = _dot_nt(qb, kb) * jnp.exp(dmat - m_t)
    st = st_ref[...]
    r_intra = _dot(s.astype(BF16), va)
    r_inter = _dot(qb, st.astype(BF16))
    w_inter = jnp.exp(inter - m_t)
    num = r_intra[:, :HEAD_DIM] + w_inter * r_inter[:, :HEAD_DIM]
    den = r_intra[:, HEAD_DIM:] + w_inter * r_inter[:, HEAD_DIM:]
    h_out = num / jnp.maximum(jnp.abs(den), jnp.exp(-m_t))
    a = g + u_row
    m_new = jnp.maximum(g + m_st, jnp.max(a, axis=-1, keepdims=True))
    ktw = (kt * jnp.exp(a - m_new)).astype(BF16)
    st_ref[...] = jnp.exp(g + m_st - m_new) * st + _dot(ktw, va)
    return h_out, m_new


def _mlstm_kernel(q_ref, k_ref, v_ref, o_ref, gt_ref, cwq_ref, cwk_ref, cbq_ref, cbk_ref, gb_ref, gain_ref,
                  out_ref, qs_ref, ks_ref, kt_ref, va_ref, bp_ref, bs_ref, ut_ref, bf_ref, bb_ref, hf_ref, hb_ref,
                  sf_ref, sb_ref):
    t_len = q_ref.shape[0]
    n_chunks = t_len // CHUNK
    head = pl.program_id(1)
    row = lax.broadcasted_iota(jnp.int32, (t_len, 1), 0)
    lane = lax.broadcasted_iota(jnp.int32, (1, LANES), 1)

    @pl.when(head == 0)
    def _():
        gates = gt_ref[...] + gb_ref[...]
        logf = _log_sigmoid(gates)
        rin = row & (CHUNK - 1)
        bp = logf
        bs = logf
        k = 1
        while k < CHUNK:
            bp = bp + jnp.where(rin >= k, pltpu.roll(bp, k, 0), 0.0)
            bs = bs + jnp.where(rin < CHUNK - k, pltpu.roll(bs, t_len - k, 0), 0.0)
            k *= 2
        bp_ref[...] = bp
        bs_ref[...] = bs
        u = gates - jnp.where(lane < 2 * H_MLSTM, pltpu.roll(bp, LANES - H_MLSTM, 1),
                              pltpu.roll(bs, LANES - H_MLSTM, 1))
        for c in range(n_chunks):
            ut_ref[:, c * CHUNK:(c + 1) * CHUNK] = u[c * CHUNK:(c + 1) * CHUNK, :].T[0:ut_ref.shape[0], :]

    def conv_silu(x, w_ref, b_ref):
        prev = jnp.where(row >= 1, pltpu.roll(x, 1, 0), 0.0)
        nxt = jnp.where(row <= t_len - 2, pltpu.roll(x, t_len - 1, 0), 0.0)
        y = prev * w_ref[0:1, :] + x * w_ref[1:2, :] + nxt * w_ref[2:3, :] + b_ref[...]
        return y * _sigmoid(y)

    def lane_bcast(x, idx):
        return jnp.broadcast_to(jnp.sum(jnp.where(lane == idx, x, 0.0), axis=-1, keepdims=True), x.shape)

    qs_ref[...] = (conv_silu(q_ref[...], cwq_ref, cbq_ref) * (HEAD_DIM ** -0.5)).astype(BF16)
    k = conv_silu(k_ref[...], cwk_ref, cbk_ref)
    ks_ref[...] = k.astype(BF16)
    for c in range(n_chunks):
        kt_ref[:, c * CHUNK:(c + 1) * CHUNK] = k[c * CHUNK:(c + 1) * CHUNK, :].T
    va_ref[:, :HEAD_DIM] = v_ref[...].astype(BF16)
    va_ref[:, HEAD_DIM:] = jnp.ones((t_len, HEAD_DIM), BF16)
    bf_ref[...] = lane_bcast(bp_ref[...], H_MLSTM + head)
    bb_ref[...] = lane_bcast(bs_ref[...], 3 * H_MLSTM + head)
    sf_ref[...] = jnp.zeros_like(sf_ref)
    sb_ref[...] = jnp.zeros_like(sb_ref)

    ti = lax.broadcasted_iota(jnp.int32, (CHUNK, CHUNK), 0)
    si = lax.broadcasted_iota(jnp.int32, (CHUNK, CHUNK), 1)
    mask_f = ti >= si
    mask_b = ti <= si
    sub = lax.broadcasted_iota(jnp.int32, (2 * H_MLSTM, 1), 0)

    def rowsel(x):
        return jnp.sum(jnp.where(sub == head, x, 0.0), axis=0, keepdims=True)

    mf = mb = jnp.full((1, 1), NEG, F32)
    for i in range(n_chunks):
        sf = slice(i * CHUNK, (i + 1) * CHUNK)
        sb = slice((n_chunks - 1 - i) * CHUNK, (n_chunks - i) * CHUNK)

        b = bf_ref[sf, :]
        hf, mf = _mlstm_step(qs_ref[sf, :], ks_ref[sf, :], kt_ref[:, sf], va_ref[sf, :], b,
                             rowsel(ut_ref[0:2 * H_MLSTM, sf]), mask_f, b[CHUNK - 1:CHUNK, 0:1], sf_ref, mf)
        hf_ref[sf, :] = hf

        b = bb_ref[sb, :]
        hb, mb = _mlstm_step(qs_ref[sb, :], ks_ref[sb, :], kt_ref[:, sb], va_ref[sb, :], b,
                             rowsel(ut_ref[2 * H_MLSTM:4 * H_MLSTM, sb]), mask_b, b[0:1, 0:1], sb_ref, mb)
        hb_ref[sb, :] = hb

    y = _sigmoid(o_ref[...]) * (hf_ref[...] + hb_ref[...])
    out_ref[...] = _rms(y, gain_ref[...]).astype(out_ref.dtype)


def _mlstm(proj, gates, conv_w, conv_b, gate_b, gain, *, batch, t_len):
    w = H_MLSTM * HEAD_DIM
    blk = lambda off: pl.BlockSpec((None, t_len, HEAD_DIM), lambda b, h: (off + h, b, 0))
    vec = lambda off: pl.BlockSpec((1, HEAD_DIM), lambda b, h: (0, off + h))
    seq = pltpu.VMEM((t_len, HEAD_DIM), F32)
    seqb = pltpu.VMEM((t_len, HEAD_DIM), BF16)
    state = pltpu.VMEM((HEAD_DIM, 2 * HEAD_DIM), F32)
    return pl.pallas_call(
        _mlstm_kernel,
        grid=(batch, H_MLSTM),
        in_specs=[
            blk(_BLK_MQ), blk(_BLK_MK), blk(_BLK_MV), blk(_BLK_MO),
            pl.BlockSpec((t_len, LANES), lambda b, h: (b, 0)),
            pl.BlockSpec((3, HEAD_DIM), lambda b, h: (0, h)),
            pl.BlockSpec((3, HEAD_DIM), lambda b, h: (0, H_MLSTM + h)),
            vec(0), vec(H_MLSTM),
            pl.BlockSpec((1, LANES), lambda b, h: (0, 0)),
            vec(0),
        ],
        out_specs=pl.BlockSpec((t_len, HEAD_DIM), lambda b, h: (b, h)),
        out_shape=jax.ShapeDtypeStruct((batch * t_len, w), BF16),
        scratch_shapes=[
            seqb, seqb, pltpu.VMEM((HEAD_DIM, t_len), F32), pltpu.VMEM((t_len, 2 * HEAD_DIM), BF16),
            seq, seq, pltpu.VMEM((4 * H_MLSTM, t_len), F32), seq, seq, seq, seq, state, state,
        ],
        compiler_params=_params("parallel", "arbitrary"),
        name="mlstm",
    )(proj, proj, proj, proj, gates, conv_w, conv_w, conv_b.reshape(1, -1), conv_b.reshape(1, -1),
      gate_b, gain.reshape(1, -1))


_NA_HEADS = 2


def _na_kernel(q_ref, k_ref, v_ref, bias_ref, out_ref, qb_ref, kb_ref, va_ref):
    t_len = q_ref.shape[1]
    rows = t_len // GRID_W
    wr = min(WIN_R, rows)
    n_keys = wr * GRID_W

    for hh in range(_NA_HEADS):
        qb_ref[hh] = (q_ref[hh] * (HEAD_DIM ** -0.5)).astype(BF16)
        kb_ref[hh] = k_ref[hh].astype(BF16)
        va_ref[hh, :, :HEAD_DIM] = v_ref[hh].astype(BF16)
        va_ref[hh, :, HEAD_DIM:] = jnp.ones((t_len, HEAD_DIM), BF16)

    for r in range(rows):
        r0 = min(max(r - wr // 2, 0), rows - wr)
        sq = slice(r * GRID_W, (r + 1) * GRID_W)
        sk = slice(r0 * GRID_W, r0 * GRID_W + n_keys)
        pairs = [WIN_R - 1 + r0 + 2 * c - r for c in range(wr // 2)]
        for hh in range(_NA_HEADS):
            bias = jnp.concatenate([bias_ref[hh, a] for a in pairs], axis=1)
            s = _dot_nt(qb_ref[hh, sq, :], kb_ref[hh, sk, :]) + bias
            e = jnp.exp(s - jnp.max(s, axis=-1, keepdims=True))
            o = _dot(e.astype(BF16), va_ref[hh, sk, :])
            out_ref[sq, hh * HEAD_DIM:(hh + 1) * HEAD_DIM] = (o[:, :HEAD_DIM] / o[:, HEAD_DIM:]).astype(out_ref.dtype)


def _na_bias_pairs(rpb):
    qc = np.arange(GRID_W)[:, None]
    kc = np.arange(GRID_W)[None, :]
    win_c0 = np.clip(qc - WIN_C // 2, 0, GRID_W - WIN_C)
    ok = (kc >= win_c0) & (kc < win_c0 + WIN_C)
    sel_c = np.arange(2 * WIN_C - 1) == (kc - qc + WIN_C - 1)[..., None]
    t = jnp.einsum("hrc,qjc->hrqj", rpb.astype(F32), jnp.asarray(sel_c, F32), precision=lax.Precision.HIGHEST)
    t = jnp.where(jnp.asarray(ok), t, NEG)
    return jnp.concatenate([t[:, :-1], t[:, 1:]], axis=-1)


def _na(proj, bias_tab, *, batch, t_len):
    assert min(WIN_R, t_len // GRID_W) % 2 == 0 and 2 * GRID_W == LANES
    assert H_NA % _NA_HEADS == 0 and _BLK_NQ % _NA_HEADS == 0 and _BLK_NK % _NA_HEADS == 0 and _BLK_NV % _NA_HEADS == 0
    blk = lambda off: pl.BlockSpec((_NA_HEADS, t_len, HEAD_DIM), lambda b, h: (off // _NA_HEADS + h, b, 0))
    seq = pltpu.VMEM((_NA_HEADS, t_len, HEAD_DIM), BF16)
    return pl.pallas_call(
        _na_kernel,
        grid=(batch, H_NA // _NA_HEADS),
        in_specs=[
            blk(_BLK_NQ), blk(_BLK_NK), blk(_BLK_NV),
            pl.BlockSpec((_NA_HEADS,) + bias_tab.shape[1:], lambda b, h: (h, 0, 0, 0)),
        ],
        out_specs=pl.BlockSpec((t_len, _NA_HEADS * HEAD_DIM), lambda b, h: (b, h)),
        out_shape=jax.ShapeDtypeStruct((batch * t_len, H_NA * HEAD_DIM), BF16),
        scratch_shapes=[seq, seq, pltpu.VMEM((_NA_HEADS, t_len, 2 * HEAD_DIM), BF16)],
        compiler_params=_params("parallel", "parallel"),
        name="natten",
    )(proj, proj, proj, bias_tab)


def _ret_kernel(q_ref, k_ref, v_ref, g_ref, cos_ref, sin_ref, dl_ref, gain_ref, out_ref,
                qb_ref, qx_ref, kb_ref, kz_ref, vb_ref, o_ref, kv_ref, r_ref):
    t_len = q_ref.shape[0]
    n_chunks = t_len // CHUNK

    def rope(x):
        return x * cos_ref[...] + pltpu.roll(x, HEAD_DIM // 2, 1) * sin_ref[...]

    lg = _log_sigmoid(dl_ref[0])
    lg_f, lg_b = lg[0:1, :], lg[1:2, :]
    diff = (lax.broadcasted_iota(jnp.int32, (CHUNK, CHUNK), 0)
            - lax.broadcasted_iota(jnp.int32, (CHUNK, CHUNK), 1)).astype(F32)
    intra = (jnp.where(diff >= 0, jnp.exp(jnp.where(diff >= 0, diff, 0.0) * lg_f), 0.0)
             + jnp.where(diff <= 0, jnp.exp(jnp.where(diff <= 0, -diff, 0.0) * lg_b), 0.0))
    gch_f = jnp.exp(CHUNK * lg_f)
    gch_b = jnp.exp(CHUNK * lg_b)
    pos = lax.broadcasted_iota(jnp.int32, (CHUNK, HEAD_DIM), 0).astype(F32)

    xi_f = jnp.exp((pos + 1.0) * lg_f)
    xi_b = jnp.exp((CHUNK - pos) * lg_b)
    zeta_f = jnp.exp((CHUNK - 1.0 - pos) * lg_f)
    zeta_b = jnp.exp(pos * lg_b)
    q = rope(q_ref[...])
    k = rope(k_ref[...]) * (HEAD_DIM ** -0.5)
    qb_ref[...] = q.astype(BF16)
    kb_ref[...] = k.astype(BF16)
    vb_ref[...] = v_ref[...].astype(BF16)
    for c in range(n_chunks):
        cs = slice(c * CHUNK, (c + 1) * CHUNK)
        qx_ref[cs, :HEAD_DIM] = (q[cs, :] * xi_f).astype(BF16)
        qx_ref[cs, HEAD_DIM:] = (q[cs, :] * xi_b).astype(BF16)
        kz_ref[:HEAD_DIM, cs] = (k[cs, :] * zeta_f).T.astype(BF16)
        kz_ref[HEAD_DIM:, cs] = (k[cs, :] * zeta_b).T.astype(BF16)

    def chunk(c):
        return slice(c * CHUNK, (c + 1) * CHUNK)

    for c in range(n_chunks):
        sl = chunk(c)
        vb = vb_ref[sl, :]
        s = _dot_nt(qb_ref[sl, :], kb_ref[sl, :]) * intra
        o_ref[sl, :] = _dot(s.astype(BF16), vb)
        kv_ref[c] = _dot(kz_ref[:, sl], vb)

    def state_pass(i, carry):
        rf, rb = carry
        cb = n_chunks - 1 - i
        r_ref[i, :HEAD_DIM, :] = rf.astype(BF16)
        r_ref[cb, HEAD_DIM:, :] = rb.astype(BF16)
        return gch_f * rf + kv_ref[i, :HEAD_DIM, :], gch_b * rb + kv_ref[cb, HEAD_DIM:, :]

    zero = jnp.zeros((HEAD_DIM, HEAD_DIM), F32)
    lax.fori_loop(0, n_chunks, state_pass, (zero, zero))

    for c in range(n_chunks):
        o_ref[chunk(c), :] += _dot(qx_ref[chunk(c), :], r_ref[c])

    g = g_ref[...]
    out_ref[...] = (g * _sigmoid(g) * _rms(o_ref[...], gain_ref[...])).astype(out_ref.dtype)


def _ret(proj, cos2, sin2, decay_rows, gain, *, batch, t_len):
    n_chunks = t_len // CHUNK
    blk = lambda off: pl.BlockSpec((None, t_len, HEAD_DIM), lambda b, h: (off + h, b, 0))
    tab = pl.BlockSpec((t_len, HEAD_DIM), lambda b, h: (0, 0))
    seqb = pltpu.VMEM((t_len, HEAD_DIM), BF16)
    return pl.pallas_call(
        _ret_kernel,
        grid=(batch, H_RET),
        in_specs=[
            blk(_BLK_RQ), blk(_BLK_RK), blk(_BLK_RV), blk(_BLK_RG), tab, tab,
            pl.BlockSpec((1, 2, LANES), lambda b, h: (h, 0, 0)),
            pl.BlockSpec((1, HEAD_DIM), lambda b, h: (0, h)),
        ],
        out_specs=pl.BlockSpec((t_len, HEAD_DIM), lambda b, h: (b, h)),
        out_shape=jax.ShapeDtypeStruct((batch * t_len, H_RET * HEAD_DIM), BF16),
        scratch_shapes=[
            seqb, pltpu.VMEM((t_len, 2 * HEAD_DIM), BF16), seqb, pltpu.VMEM((2 * HEAD_DIM, t_len), BF16), seqb,
            pltpu.VMEM((t_len, HEAD_DIM), F32),
            pltpu.VMEM((n_chunks, 2 * HEAD_DIM, HEAD_DIM), F32),
            pltpu.VMEM((n_chunks, 2 * HEAD_DIM, HEAD_DIM), BF16),
        ],
        compiler_params=_params("parallel", "parallel"),
        name="retention",
    )(proj, proj, proj, proj, cos2, sin2, decay_rows, gain.reshape(1, -1))


def _rope_tables(t_len):
    inv = ROPE_BASE ** (-jnp.arange(0, HEAD_DIM, 2, dtype=F32) / HEAD_DIM)
    ang = jnp.arange(t_len, dtype=F32)[:, None] * inv[None, :]
    cos, sin = jnp.cos(ang), jnp.sin(ang)
    return jnp.concatenate([cos, cos], axis=-1), jnp.concatenate([-sin, sin], axis=-1)


def _pick(n, pref):
    return pref if n % pref == 0 else n


def kernel(x, ffn1_norm, ffn1_w_gu, ffn1_w_down, mix_norm, w_in, mlstm_conv_w, mlstm_conv_b, mlstm_gate_b,
           mlstm_head_norm, na_rpb, ret_decay_logit, ret_head_norm, w_out, ffn2_norm, ffn2_w_gu, ffn2_w_down,
           final_norm):
    batch, t_len, d = x.shape
    n = batch * t_len
    depth = ffn1_norm.shape[0]
    n_gates = 4 * H_MLSTM
    gate_lo = 4 * H_MLSTM * HEAD_DIM

    tm = _pick(n, 1024)
    ffn = functools.partial(_ffn, tm=tm, tf=512)
    cos2, sin2 = _rope_tables(t_len)

    h = x.reshape(n, d)
    w_gu1, w_down1 = _cast_layer(ffn1_w_gu, 0, tr=256), _cast_layer(ffn1_w_down, 0, tr=704)
    for l in range(depth):
        h, (w_gu2, w_down2) = ffn(h, ffn1_norm[l], w_gu1, w_down1, final_norm, final_norm=False,
                                  cast=((ffn2_w_gu, l), (ffn2_w_down, l)))

        w_main, w_gate = _cast_w_in(w_in, l, gate_lo=gate_lo, n_gates=n_gates, tc=512)
        proj, gates, wo = _mix_in(h, mix_norm[l], w_main, w_gate, w_out, l, tm=tm, tn=1792)

        gate_b = jnp.pad(mlstm_gate_b[l].astype(F32).reshape(1, n_gates), ((0, 0), (0, LANES - n_gates)))
        m_out = _mlstm(proj, gates, mlstm_conv_w[l], mlstm_conv_b[l], gate_b, mlstm_head_norm[l],
                       batch=batch, t_len=t_len)
        n_out = _na(proj, _na_bias_pairs(na_rpb[l]), batch=batch, t_len=t_len)
        decay_rows = jnp.broadcast_to(ret_decay_logit[l].astype(F32).T[:, :, None], (H_RET, 2, LANES))
        r_out = _ret(proj, cos2, sin2, decay_rows, ret_head_norm[l], batch=batch, t_len=t_len)

        h = _mix_out(h, m_out, n_out, r_out, wo, tm=_pick(n, 512), tn=d)

        last = l == depth - 1
        h, nxt = ffn(h, ffn2_norm[l], w_gu2, w_down2, final_norm, final_norm=last,
                     cast=() if last else ((ffn1_w_gu, l + 1), (ffn1_w_down, l + 1)))
        if not last:
            w_gu1, w_down1 = nxt
    return h.reshape(batch, t_len, d)
```

```python
import functools

import numpy as np
import jax
import jax.numpy as jnp
from jax import lax
from jax.experimental import pallas as pl
from jax.experimental.pallas import tpu as pltpu

F32 = jnp.float32
BF16 = jnp.bfloat16

HEAD_DIM = 128
H_MLSTM = 4
H_NA = 8
H_RET = 4
CHUNK = 128
GRID_W = 64
WIN_R = 8
WIN_C = 16
ROPE_BASE = 10000.0
EPS = 1e-6
NEG = -1e30

LANES = 128
VMEM_BYTES = 64 * 1024 * 1024
VMEM_LIMIT_BYTES = VMEM_BYTES - 8 * 1024 * 1024
FFN_VMEM_LIMIT_BYTES = VMEM_BYTES - 3 * 1024 * 1024

_BLK_MQ, _BLK_MK, _BLK_MV, _BLK_MO = 0, 4, 8, 12
_BLK_NQ, _BLK_NK, _BLK_NV = 16, 24, 32
_BLK_RQ, _BLK_RK, _BLK_RV, _BLK_RG = 40, 44, 48, 52


def _params(*sem, vmem=VMEM_LIMIT_BYTES):
    return pltpu.CompilerParams(dimension_semantics=sem, vmem_limit_bytes=vmem)


def _sigmoid(x):
    return 1.0 / (1.0 + jnp.exp(-x))


def _log_sigmoid(x):
    return jnp.minimum(x, 0.0) - jnp.log1p(jnp.exp(-jnp.abs(x)))


def _rms(x, g):
    return x * lax.rsqrt(jnp.mean(x * x, axis=-1, keepdims=True) + EPS) * g


def _dot(a, b):
    return jnp.dot(a, b, preferred_element_type=F32)


def _dot_nt(a, b):
    return lax.dot_general(a, b, (((1,), (1,)), ((), ())), preferred_element_type=F32)


def _cast_kernel(w_ref, o_ref):
    o_ref[...] = w_ref[...].astype(o_ref.dtype)


def _cast_layer(w, layer, *, tr):
    _, r, c = w.shape
    return pl.pallas_call(
        _cast_kernel,
        grid=(r // tr,),
        in_specs=[pl.BlockSpec((None, tr, c), lambda i: (layer, i, 0))],
        out_specs=pl.BlockSpec((tr, c), lambda i: (i, 0)),
        out_shape=jax.ShapeDtypeStruct((r, c), BF16),
        compiler_params=_params("parallel"),
        name="cast_weight",
    )(w)


def _cast_w_in_kernel(wt_ref, gt_ref, main_ref, gate_ref, *, n_gates):
    main_ref[...] = wt_ref[0].T.astype(BF16)

    @pl.when(pl.program_id(0) == 0)
    def _():
        gate_ref[...] = jnp.zeros_like(gate_ref)
        gate_ref[:, :n_gates] = gt_ref[0].T.astype(BF16)


def _cast_w_in(w_in, layer, *, gate_lo, n_gates, tc):
    wt = jnp.swapaxes(w_in, 1, 2)
    _, c, d = wt.shape
    n_main = c - n_gates

    def src_rows(i):
        return pl.multiple_of(jnp.where(i * tc >= gate_lo, i * tc + n_gates, i * tc), n_gates)

    return pl.pallas_call(
        functools.partial(_cast_w_in_kernel, n_gates=n_gates),
        grid=(n_main // tc,),
        in_specs=[
            pl.BlockSpec((pl.Element(1), pl.Element(tc), pl.Element(d)), lambda i: (layer, src_rows(i), 0)),
            pl.BlockSpec((pl.Element(1), pl.Element(n_gates), pl.Element(d)), lambda i: (layer, gate_lo, 0)),
        ],
        out_specs=[pl.BlockSpec((d, tc), lambda i: (0, i)), pl.BlockSpec((d, LANES), lambda i: (0, 0))],
        out_shape=[jax.ShapeDtypeStruct((d, n_main), BF16), jax.ShapeDtypeStruct((d, LANES), BF16)],
        compiler_params=_params("arbitrary"),
        name="cast_w_in",
    )(wt, wt)


_FFN_ROWS = 256


def _ffn_kernel(x_ref, g_ref, wg_ref, wu_ref, wd_ref, fg_ref, *rest, final_norm, n_cast):
    cast_in, (o_ref, *cast_out), xn_ref = rest[:n_cast], rest[n_cast:2 * n_cast + 1], rest[2 * n_cast + 1]
    j = pl.program_id(1)
    last = pl.num_programs(1) - 1
    tm, d = o_ref.shape

    def step(first, final):
        if first:
            for r in range(0, tm, _FFN_ROWS):
                xn_ref[r:r + _FFN_ROWS, :] = _rms(x_ref[r:r + _FFN_ROWS, :], g_ref[...]).astype(BF16)
        xn = xn_ref[...]
        gate = _dot(xn, wg_ref[...])
        up = _dot(xn, wu_ref[...])
        act = (gate * _sigmoid(gate) * up).astype(BF16)
        down = _dot(act, wd_ref[...])
        if first:
            o_ref[...] = down
        elif final and not final_norm:
            o_ref[...] = x_ref[...] + 0.5 * (o_ref[...] + down)
        else:
            o_ref[...] += down
        for src, dst in zip(cast_in, cast_out):
            dst[...] = src[...].astype(dst.dtype)

    pl.when(j == 0)(lambda: step(True, False))
    pl.when(jnp.logical_and(j > 0, j < last))(lambda: step(False, False))
    pl.when(j == last)(lambda: step(False, True))

    if final_norm:
        @pl.when(j == last)
        def _():
            def finish_rows(c, carry):
                rows = pl.ds(pl.multiple_of(c * _FFN_ROWS, _FFN_ROWS), _FFN_ROWS)
                o_ref[rows, :] = _rms(x_ref[rows, :] + 0.5 * o_ref[rows, :], fg_ref[...])
                return carry
            lax.fori_loop(0, tm // _FFN_ROWS, finish_rows, 0)


def _cast_plan(w, n_i, n_j):
    _, r, c = w.shape
    steps = n_i * n_j
    if r % n_i == 0 and c % n_j == 0 and (c // n_j) % LANES == 0 and (r // n_i) % 16 == 0:
        return (r // n_i, c // n_j), (lambda i, j: (i, j))
    assert r % steps == 0 and (r // steps) % 16 == 0, (w.shape, n_i, n_j)
    return (r // steps, c), (lambda i, j: (i * n_j + j, 0))


def _ffn(h, g, w_gu, w_down, final_g, *, final_norm, tm, tf, cast=()):
    n, d = h.shape
    f = w_down.shape[0]
    nf = f // tf
    grid = (n // tm, nf)
    cast_specs_in, cast_specs_out, cast_shapes = [], [], []
    for w, layer in cast:
        blk, imap = _cast_plan(w, *grid)
        cast_specs_in.append(pl.BlockSpec((None,) + blk, functools.partial(lambda i, j, m, l: (l,) + m(i, j), m=imap, l=layer)))
        cast_specs_out.append(pl.BlockSpec(blk, imap))
        cast_shapes.append(jax.ShapeDtypeStruct(w.shape[1:], BF16))
    outs = pl.pallas_call(
        functools.partial(_ffn_kernel, final_norm=final_norm, n_cast=len(cast)),
        grid=grid,
        in_specs=[
            pl.BlockSpec((tm, d), lambda i, j: (i, 0)),
            pl.BlockSpec((1, d), lambda i, j: (0, 0)),
            pl.BlockSpec((d, tf), lambda i, j: (0, j)),
            pl.BlockSpec((d, tf), lambda i, j: (0, j + nf)),
            pl.BlockSpec((tf, d), lambda i, j: (j, 0)),
            pl.BlockSpec((1, d), lambda i, j: (0, 0)),
        ] + cast_specs_in,
        out_specs=[pl.BlockSpec((tm, d), lambda i, j: (i, 0))] + cast_specs_out,
        out_shape=[jax.ShapeDtypeStruct((n, d), F32)] + cast_shapes,
        scratch_shapes=[pltpu.VMEM((tm, d), BF16)],
        compiler_params=_params("parallel", "arbitrary", vmem=FFN_VMEM_LIMIT_BYTES),
        name="ffn",
    )(h, g.reshape(1, d), w_gu, w_gu, w_down, final_g.reshape(1, d), *[w for w, _ in cast])
    return outs[0], outs[1:]


def _mix_in_kernel(x_ref, g_ref, w_ref, wgate_ref, wo_ref, o_ref, og_ref, wob_ref, xn_ref):
    def step(first):
        if first:
            rows = _FFN_ROWS
            for r in range(0, xn_ref.shape[0], rows):
                xn_ref[r:r + rows, :] = _rms(x_ref[r:r + rows, :], g_ref[...]).astype(BF16)
            og_ref[...] = _dot(xn_ref[...], wgate_ref[...])
        res = _dot(xn_ref[...], w_ref[...])
        for k in range(o_ref.shape[0]):
            o_ref[k] = res[:, k * HEAD_DIM:(k + 1) * HEAD_DIM]
        wob_ref[...] = wo_ref[...].astype(BF16)

    j = pl.program_id(1)
    pl.when(j == 0)(lambda: step(True))
    pl.when(j > 0)(lambda: step(False))


def _mix_in(h, g, w_main, w_gate, w_out, layer, *, tm, tn):
    n, d = h.shape
    c = w_main.shape[1]
    nb = tn // HEAD_DIM
    grid = (n // tm, c // tn)
    wo_blk, wo_map = _cast_plan(w_out, *grid)
    return pl.pallas_call(
        _mix_in_kernel,
        grid=grid,
        in_specs=[
            pl.BlockSpec((tm, d), lambda i, j: (i, 0)),
            pl.BlockSpec((1, d), lambda i, j: (0, 0)),
            pl.BlockSpec((d, tn), lambda i, j: (0, j)),
            pl.BlockSpec((d, LANES), lambda i, j: (0, 0)),
            pl.BlockSpec((None,) + wo_blk, lambda i, j: (layer,) + wo_map(i, j)),
        ],
        out_specs=[
            pl.BlockSpec((nb, tm, HEAD_DIM), lambda i, j: (j, i, 0)),
            pl.BlockSpec((tm, LANES), lambda i, j: (i, 0)),
            pl.BlockSpec(wo_blk, wo_map),
        ],
        out_shape=[jax.ShapeDtypeStruct((c // HEAD_DIM, n, HEAD_DIM), F32), jax.ShapeDtypeStruct((n, LANES), F32),
                   jax.ShapeDtypeStruct(w_out.shape[1:], BF16)],
        scratch_shapes=[pltpu.VMEM((tm, d), BF16)],
        compiler_params=_params("parallel", "arbitrary"),
        name="mix_in",
    )(h, g.reshape(1, d), w_main, w_gate, w_out)


def _mix_out_kernel(h_ref, m_ref, na_ref, nb_ref, r_ref, wm_ref, wna_ref, wnb_ref, wr_ref, o_ref):
    acc = (_dot(m_ref[...], wm_ref[...]) + _dot(na_ref[...], wna_ref[...])
           + _dot(nb_ref[...], wnb_ref[...]) + _dot(r_ref[...], wr_ref[...]))
    o_ref[...] = h_ref[...] + acc


def _mix_out(h, m_out, n_out, r_out, w_out, *, tm, tn):
    n, d = h.shape
    kb = m_out.shape[1]
    assert n_out.shape[1] == 2 * kb and r_out.shape[1] == kb and w_out.shape[0] == 4 * kb
    act = lambda c: pl.BlockSpec((tm, kb), lambda i, j: (i, c))
    wblk = lambda r: pl.BlockSpec((kb, tn), lambda i, j: (r, j))
    return pl.pallas_call(
        _mix_out_kernel,
        grid=(n // tm, d // tn),
        in_specs=[pl.BlockSpec((tm, tn), lambda i, j: (i, j)), act(0), act(0), act(1), act(0),
                  wblk(0), wblk(1), wblk(2), wblk(3)],
        out_specs=pl.BlockSpec((tm, tn), lambda i, j: (i, j)),
        out_shape=jax.ShapeDtypeStruct((n, d), F32),
        compiler_params=_params("parallel", "arbitrary"),
        name="mix_out",
    )(h, m_out, n_out, n_out, r_out, w_out, w_out, w_out, w_out)


def _mlstm_step(qb, kb, kt, va, b, u_row, mask, g, st_ref, m_st):
    dmat = jnp.where(mask, b + u_row, NEG)
    inter = b + m_st
    m_t = jnp.maximum(inter, jnp.max(dmat, axis=-1, keepdims=True))
    s = _dot_nt(qb, kb) * jnp.exp(dmat - m_t)
    st = st_ref[...]
    r_intra = _dot(s.astype(BF16), va)
    r_inter = _dot(qb, st.astype(BF16))
    w_inter = jnp.exp(inter - m_t)
    num = r_intra[:, :HEAD_DIM] + w_inter * r_inter[:, :HEAD_DIM]
    den = r_intra[:, HEAD_DIM:] + w_inter * r_inter[:, HEAD_DIM:]
    h_out = num / jnp.maximum(jnp.abs(den), jnp.exp(-m_t))
    a = g + u_row
    m_new = jnp.maximum(g + m_st, jnp.max(a, axis=-1, keepdims=True))
    ktw = (kt * jnp.exp(a - m_new)).astype(BF16)
    st_ref[...] = jnp.exp(g + m_st - m_new) * st + _dot(ktw, va)
    return h_out, m_new


def _mlstm_kernel(q_ref, k_ref, v_ref, o_ref, gt_ref, cwq_ref, cwk_ref, cbq_ref, cbk_ref, gb_ref, gain_ref,
                  out_ref, qs_ref, ks_ref, kt_ref, va_ref, bp_ref, bs_ref, ut_ref, bf_ref, bb_ref, hf_ref, hb_ref,
                  sf_ref, sb_ref):
    t_len = q_ref.shape[0]
    n_chunks = t_len // CHUNK
    head = pl.program_id(1)
    row = lax.broadcasted_iota(jnp.int32, (t_len, 1), 0)
    lane = lax.broadcasted_iota(jnp.int32, (1, LANES), 1)

    @pl.when(head == 0)
    def _():
        gates = gt_ref[...] + gb_ref[...]
        logf = _log_sigmoid(gates)
        rin = row & (CHUNK - 1)
        bp = logf
        bs = logf
        k = 1
        while k < CHUNK:
            bp = bp + jnp.where(rin >= k, pltpu.roll(bp, k, 0), 0.0)
            bs = bs + jnp.where(rin < CHUNK - k, pltpu.roll(bs, t_len - k, 0), 0.0)
            k *= 2
        bp_ref[...] = bp
        bs_ref[...] = bs
        u = gates - jnp.where(lane < 2 * H_MLSTM, pltpu.roll(bp, LANES - H_MLSTM, 1),
                              pltpu.roll(bs, LANES - H_MLSTM, 1))
        for c in range(n_chunks):
            ut_ref[:, c * CHUNK:(c + 1) * CHUNK] = u[c * CHUNK:(c + 1) * CHUNK, :].T[0:ut_ref.shape[0], :]

    def conv_silu(x, w_ref, b_ref):
        prev = jnp.where(row >= 1, pltpu.roll(x, 1, 0), 0.0)
        nxt = jnp.where(row <= t_len - 2, pltpu.roll(x, t_len - 1, 0), 0.0)
        y = prev * w_ref[0:1, :] + x * w_ref[1:2, :] + nxt * w_ref[2:3, :] + b_ref[...]
        return y * _sigmoid(y)

    def lane_bcast(x, idx):
        return jnp.broadcast_to(jnp.sum(jnp.where(lane == idx, x, 0.0), axis=-1, keepdims=True), x.shape)

    qs_ref[...] = (conv_silu(q_ref[...], cwq_ref, cbq_ref) * (HEAD_DIM ** -0.5)).astype(BF16)
    k = conv_silu(k_ref[...], cwk_ref, cbk_ref)
    ks_ref[...] = k.astype(BF16)
    for c in range(n_chunks):
        kt_ref[:, c * CHUNK:(c + 1) * CHUNK] = k[c * CHUNK:(c + 1) * CHUNK, :].T
    va_ref[:, :HEAD_DIM] = v_ref[...].astype(BF16)
    va_ref[:, HEAD_DIM:] = jnp.ones((t_len, HEAD_DIM), BF16)
    bf_ref[...] = lane_bcast(bp_ref[...], H_MLSTM + head)
    bb_ref[...] = lane_bcast(bs_ref[...], 3 * H_MLSTM + head)
    sf_ref[...] = jnp.zeros_like(sf_ref)
    sb_ref[...] = jnp.zeros_like(sb_ref)

    ti = lax.broadcasted_iota(jnp.int32, (CHUNK, CHUNK), 0)
    si = lax.broadcasted_iota(jnp.int32, (CHUNK, CHUNK), 1)
    mask_f = ti >= si
    mask_b = ti <= si
    sub = lax.broadcasted_iota(jnp.int32, (2 * H_MLSTM, 1), 0)

    def rowsel(x):
        return jnp.sum(jnp.where(sub == head, x, 0.0), axis=0, keepdims=True)

    mf = mb = jnp.full((1, 1), NEG, F32)
    for i in range(n_chunks):
        sf = slice(i * CHUNK, (i + 1) * CHUNK)
        sb = slice((n_chunks - 1 - i) * CHUNK, (n_chunks - i) * CHUNK)

        b = bf_ref[sf, :]
        hf, mf = _mlstm_step(qs_ref[sf, :], ks_ref[sf, :], kt_ref[:, sf], va_ref[sf, :], b,
                             rowsel(ut_ref[0:2 * H_MLSTM, sf]), mask_f, b[CHUNK - 1:CHUNK, 0:1], sf_ref, mf)
        hf_ref[sf, :] = hf

        b = bb_ref[sb, :]
        hb, mb = _mlstm_step(qs_ref[sb, :], ks_ref[sb, :], kt_ref[:, sb], va_ref[sb, :], b,
                             rowsel(ut_ref[2 * H_MLSTM:4 * H_MLSTM, sb]), mask_b, b[0:1, 0:1], sb_ref, mb)
        hb_ref[sb, :] = hb

    y = _sigmoid(o_ref[...]) * (hf_ref[...] + hb_ref[...])
    out_ref[...] = _rms(y, gain_ref[...]).astype(out_ref.dtype)


def _mlstm(proj, gates, conv_w, conv_b, gate_b, gain, *, batch, t_len):
    w = H_MLSTM * HEAD_DIM
    blk = lambda off: pl.BlockSpec((None, t_len, HEAD_DIM), lambda b, h: (off + h, b, 0))
    vec = lambda off: pl.BlockSpec((1, HEAD_DIM), lambda b, h: (0, off + h))
    seq = pltpu.VMEM((t_len, HEAD_DIM), F32)
    seqb = pltpu.VMEM((t_len, HEAD_DIM), BF16)
    state = pltpu.VMEM((HEAD_DIM, 2 * HEAD_DIM), F32)
    return pl.pallas_call(
        _mlstm_kernel,
        grid=(batch, H_MLSTM),
        in_specs=[
            blk(_BLK_MQ), blk(_BLK_MK), blk(_BLK_MV), blk(_BLK_MO),
            pl.BlockSpec((t_len, LANES), lambda b, h: (b, 0)),
            pl.BlockSpec((3, HEAD_DIM), lambda b, h: (0, h)),
            pl.BlockSpec((3, HEAD_DIM), lambda b, h: (0, H_MLSTM + h)),
            vec(0), vec(H_MLSTM),
            pl.BlockSpec((1, LANES), lambda b, h: (0, 0)),
            vec(0),
        ],
        out_specs=pl.BlockSpec((t_len, HEAD_DIM), lambda b, h: (b, h)),
        out_shape=jax.ShapeDtypeStruct((batch * t_len, w), BF16),
        scratch_shapes=[
            seqb, seqb, pltpu.VMEM((HEAD_DIM, t_len), F32), pltpu.VMEM((t_len, 2 * HEAD_DIM), BF16),
            seq, seq, pltpu.VMEM((4 * H_MLSTM, t_len), F32), seq, seq, seq, seq, state, state,
        ],
        compiler_params=_params("parallel", "arbitrary"),
        name="mlstm",
    )(proj, proj, proj, proj, gates, conv_w, conv_w, conv_b.reshape(1, -1), conv_b.reshape(1, -1),
      gate_b, gain.reshape(1, -1))


_NA_HEADS = 2


def _na_kernel(q_ref, k_ref, v_ref, bias_ref, out_ref, qb_ref, kb_ref, va_ref):
    t_len = q_ref.shape[1]
    rows = t_len // GRID_W
    wr = min(WIN_R, rows)
    n_keys = wr * GRID_W

    for hh in range(_NA_HEADS):
        qb_ref[hh] = (q_ref[hh] * (HEAD_DIM ** -0.5)).astype(BF16)
        kb_ref[hh] = k_ref[hh].astype(BF16)
        va_ref[hh, :, :HEAD_DIM] = v_ref[hh].astype(BF16)
        va_ref[hh, :, HEAD_DIM:] = jnp.ones((t_len, HEAD_DIM), BF16)

    for r in range(rows):
        r0 = min(max(r - wr // 2, 0), rows - wr)
        sq = slice(r * GRID_W, (r + 1) * GRID_W)
        sk = slice(r0 * GRID_W, r0 * GRID_W + n_keys)
        pairs = [WIN_R - 1 + r0 + 2 * c - r for c in range(wr // 2)]
        for hh in range(_NA_HEADS):
            bias = jnp.concatenate([bias_ref[hh, a] for a in pairs], axis=1)
            s = _dot_nt(qb_ref[hh, sq, :], kb_ref[hh, sk, :]) + bias
            e = jnp.exp(s - jnp.max(s, axis=-1, keepdims=True))
            o = _dot(e.astype(BF16), va_ref[hh, sk, :])
            out_ref[sq, hh * HEAD_DIM:(hh + 1) * HEAD_DIM] = (o[:, :HEAD_DIM] / o[:, HEAD_DIM:]).astype(out_ref.dtype)


def _na_bias_pairs(rpb):
    qc = np.arange(GRID_W)[:, None]
    kc = np.arange(GRID_W)[None, :]
    win_c0 = np.clip(qc - WIN_C // 2, 0, GRID_W - WIN_C)
    ok = (kc >= win_c0) & (kc < win_c0 + WIN_C)
    sel_c = np.arange(2 * WIN_C - 1) == (kc - qc + WIN_C - 1)[..., None]
    t = jnp.einsum("hrc,qjc->hrqj", rpb.astype(F32), jnp.asarray(sel_c, F32), precision=lax.Precision.HIGHEST)
    t = jnp.where(jnp.asarray(ok), t, NEG)
    return jnp.concatenate([t[:, :-1], t[:, 1:]], axis=-1)


def _na(proj, bias_tab, *, batch, t_len):
    assert min(WIN_R, t_len // GRID_W) % 2 == 0 and 2 * GRID_W == LANES
    assert H_NA % _NA_HEADS == 0 and _BLK_NQ % _NA_HEADS == 0 and _BLK_NK % _NA_HEADS == 0 and _BLK_NV % _NA_HEADS == 0
    blk = lambda off: pl.BlockSpec((_NA_HEADS, t_len, HEAD_DIM), lambda b, h: (off // _NA_HEADS + h, b, 0))
    seq = pltpu.VMEM((_NA_HEADS, t_len, HEAD_DIM), BF16)
    return pl.pallas_call(
        _na_kernel,
        grid=(batch, H_NA // _NA_HEADS),
        in_specs=[
            blk(_BLK_NQ), blk(_BLK_NK), blk(_BLK_NV),
            pl.BlockSpec((_NA_HEADS,) + bias_tab.shape[1:], lambda b, h: (h, 0, 0, 0)),
        ],
        out_specs=pl.BlockSpec((t_len, _NA_HEADS * HEAD_DIM), lambda b, h: (b, h)),
        out_shape=jax.ShapeDtypeStruct((batch * t_len, H_NA * HEAD_DIM), BF16),
        scratch_shapes=[seq, seq, pltpu.VMEM((_NA_HEADS, t_len, 2 * HEAD_DIM), BF16)],
        compiler_params=_params("parallel", "parallel"),
        name="natten",
    )(proj, proj, proj, bias_tab)


def _ret_kernel(q_ref, k_ref, v_ref, g_ref, cos_ref, sin_ref, dl_ref, gain_ref, out_ref,
                qb_ref, qx_ref, kb_ref, kz_ref, vb_ref, o_ref, kv_ref, r_ref):
    t_len = q_ref.shape[0]
    n_chunks = t_len // CHUNK

    def rope(x):
        return x * cos_ref[...] + pltpu.roll(x, HEAD_DIM // 2, 1) * sin_ref[...]

    lg = _log_sigmoid(dl_ref[0])
    lg_f, lg_b = lg[0:1, :], lg[1:2, :]
    diff = (lax.broadcasted_iota(jnp.int32, (CHUNK, CHUNK), 0)
            - lax.broadcasted_iota(jnp.int32, (CHUNK, CHUNK), 1)).astype(F32)
    intra = (jnp.where(diff >= 0, jnp.exp(jnp.where(diff >= 0, diff, 0.0) * lg_f), 0.0)
             + jnp.where(diff <= 0, jnp.exp(jnp.where(diff <= 0, -diff, 0.0) * lg_b), 0.0))
    gch_f = jnp.exp(CHUNK * lg_f)
    gch_b = jnp.exp(CHUNK * lg_b)
    pos = lax.broadcasted_iota(jnp.int32, (CHUNK, HEAD_DIM), 0).astype(F32)

    xi_f = jnp.exp((pos + 1.0) * lg_f)
    xi_b = jnp.exp((CHUNK - pos) * lg_b)
    zeta_f = jnp.exp((CHUNK - 1.0 - pos) * lg_f)
    zeta_b = jnp.exp(pos * lg_b)
    q = rope(q_ref[...])
    k = rope(k_ref[...]) * (HEAD_DIM ** -0.5)
    qb_ref[...] = q.astype(BF16)
    kb_ref[...] = k.astype(BF16)
    vb_ref[...] = v_ref[...].astype(BF16)
    for c in range(n_chunks):
        cs = slice(c * CHUNK, (c + 1) * CHUNK)
        qx_ref[cs, :HEAD_DIM] = (q[cs, :] * xi_f).astype(BF16)
        qx_ref[cs, HEAD_DIM:] = (q[cs, :] * xi_b).astype(BF16)
        kz_ref[:HEAD_DIM, cs] = (k[cs, :] * zeta_f).T.astype(BF16)
        kz_ref[HEAD_DIM:, cs] = (k[cs, :] * zeta_b).T.astype(BF16)

    def chunk(c):
        return slice(c * CHUNK, (c + 1) * CHUNK)

    for c in range(n_chunks):
        sl = chunk(c)
        vb = vb_ref[sl, :]
        s = _dot_nt(qb_ref[sl, :], kb_ref[sl, :]) * intra
        o_ref[sl, :] = _dot(s.astype(BF16), vb)
        kv_ref[c] = _dot(kz_ref[:, sl], vb)

    def state_pass(i, carry):
        rf, rb = carry
        cb = n_chunks - 1 - i
        r_ref[i, :HEAD_DIM, :] = rf.astype(BF16)
        r_ref[cb, HEAD_DIM:, :] = rb.astype(BF16)
        return gch_f * rf + kv_ref[i, :HEAD_DIM, :], gch_b * rb + kv_ref[cb, HEAD_DIM:, :]

    zero = jnp.zeros((HEAD_DIM, HEAD_DIM), F32)
    lax.fori_loop(0, n_chunks, state_pass, (zero, zero))

    for c in range(n_chunks):
        o_ref[chunk(c), :] += _dot(qx_ref[chunk(c), :], r_ref[c])

    g = g_ref[...]
    out_ref[...] = (g * _sigmoid(g) * _rms(o_ref[...], gain_ref[...])).astype(out_ref.dtype)


def _ret(proj, cos2, sin2, decay_rows, gain, *, batch, t_len):
    n_chunks = t_len // CHUNK
    blk = lambda off: pl.BlockSpec((None, t_len, HEAD_DIM), lambda b, h: (off + h, b, 0))
    tab = pl.BlockSpec((t_len, HEAD_DIM), lambda b, h: (0, 0))
    seqb = pltpu.VMEM((t_len, HEAD_DIM), BF16)
    return pl.pallas_call(
        _ret_kernel,
        grid=(batch, H_RET),
        in_specs=[
            blk(_BLK_RQ), blk(_BLK_RK), blk(_BLK_RV), blk(_BLK_RG), tab, tab,
            pl.BlockSpec((1, 2, LANES), lambda b, h: (h, 0, 0)),
            pl.BlockSpec((1, HEAD_DIM), lambda b, h: (0, h)),
        ],
        out_specs=pl.BlockSpec((t_len, HEAD_DIM), lambda b, h: (b, h)),
        out_shape=jax.ShapeDtypeStruct((batch * t_len, H_RET * HEAD_DIM), BF16),
        scratch_shapes=[
            seqb, pltpu.VMEM((t_len, 2 * HEAD_DIM), BF16), seqb, pltpu.VMEM((2 * HEAD_DIM, t_len), BF16), seqb,
            pltpu.VMEM((t_len, HEAD_DIM), F32),
            pltpu.VMEM((n_chunks, 2 * HEAD_DIM, HEAD_DIM), F32),
            pltpu.VMEM((n_chunks, 2 * HEAD_DIM, HEAD_DIM), BF16),
        ],
        compiler_params=_params("parallel", "parallel"),
        name="retention",
    )(proj, proj, proj, proj, cos2, sin2, decay_rows, gain.reshape(1, -1))


def _rope_tables(t_len):
    inv = ROPE_BASE ** (-jnp.arange(0, HEAD_DIM, 2, dtype=F32) / HEAD_DIM)
    ang = jnp.arange(t_len, dtype=F32)[:, None] * inv[None, :]
    cos, sin = jnp.cos(ang), jnp.sin(ang)
    return jnp.concatenate([cos, cos], axis=-1), jnp.concatenate([-sin, sin], axis=-1)


def _pick(n, pref):
    return pref if n % pref == 0 else n


def kernel(x, ffn1_norm, ffn1_w_gu, ffn1_w_down, mix_norm, w_in, mlstm_conv_w, mlstm_conv_b, mlstm_gate_b,
           mlstm_head_norm, na_rpb, ret_decay_logit, ret_head_norm, w_out, ffn2_norm, ffn2_w_gu, ffn2_w_down,
           final_norm):
    batch, t_len, d = x.shape
    n = batch * t_len
    depth = ffn1_norm.shape[0]
    n_gates = 4 * H_MLSTM
    gate_lo = 4 * H_MLSTM * HEAD_DIM

    tm = _pick(n, 1024)
    ffn = functools.partial(_ffn, tm=tm, tf=512)
    cos2, sin2 = _rope_tables(t_len)

    h = x.reshape(n, d)
    w_gu1, w_down1 = _cast_layer(ffn1_w_gu, 0, tr=256), _cast_layer(ffn1_w_down, 0, tr=704)
    for l in range(depth):
        h, (w_gu2, w_down2) = ffn(h, ffn1_norm[l], w_gu1, w_down1, final_norm, final_norm=False,
                                  cast=((ffn2_w_gu, l), (ffn2_w_down, l)))

        w_main, w_gate = _cast_w_in(w_in, l, gate_lo=gate_lo, n_gates=n_gates, tc=512)
        proj, gates, wo = _mix_in(h, mix_norm[l], w_main, w_gate, w_out, l, tm=tm, tn=1792)

        gate_b = jnp.pad(mlstm_gate_b[l].astype(F32).reshape(1, n_gates), ((0, 0), (0, LANES - n_gates)))
        m_out = _mlstm(proj, gates, mlstm_conv_w[l], mlstm_conv_b[l], gate_b, mlstm_head_norm[l],
                       batch=batch, t_len=t_len)
        n_out = _na(proj, _na_bias_pairs(na_rpb[l]), batch=batch, t_len=t_len)
        decay_rows = jnp.broadcast_to(ret_decay_logit[l].astype(F32).T[:, :, None], (H_RET, 2, LANES))
        r_out = _ret(proj, cos2, sin2, decay_rows, ret_head_norm[l], batch=batch, t_len=t_len)

        h = _mix_out(h, m_out, n_out, r_out, wo, tm=_pick(n, 512), tn=d)

        last = l == depth - 1
        h, nxt = ffn(h, ffn2_norm[l], w_gu2, w_down2, final_norm, final_norm=last,
                     cast=() if last else ((ffn1_w_gu, l + 1), (ffn1_w_down, l + 1)))
        if not last:
            w_gu1, w_down1 = nxt
    return h.reshape(batch, t_len, d)
```

```python
import functools

import numpy as np
import jax
import jax.numpy as jnp
from jax import lax
from jax.experimental import pallas as pl
from jax.experimental.pallas import tpu as pltpu

F32 = jnp.float32
BF16 = jnp.bfloat16

HEAD_DIM = 128
H_MLSTM = 4
H_NA = 8
H_RET = 4
CHUNK = 128
GRID_W = 64
WIN_R = 8
WIN_C = 16
ROPE_BASE = 10000.0
EPS = 1e-6
NEG = -1e30

LANES = 128
VMEM_BYTES = 64 * 1024 * 1024
VMEM_LIMIT_BYTES = VMEM_BYTES - 8 * 1024 * 1024
FFN_VMEM_LIMIT_BYTES = VMEM_BYTES - 3 * 1024 * 1024

_BLK_MQ, _BLK_MK, _BLK_MV, _BLK_MO = 0, 4, 8, 12
_BLK_NQ, _BLK_NK, _BLK_NV = 16, 24, 32
_BLK_RQ, _BLK_RK, _BLK_RV, _BLK_RG = 40, 44, 48, 52


def _params(*sem, vmem=VMEM_LIMIT_BYTES):
    return pltpu.CompilerParams(dimension_semantics=sem, vmem_limit_bytes=vmem)


def _sigmoid(x):
    return 1.0 / (1.0 + jnp.exp(-x))


def _log_sigmoid(x):
    return jnp.minimum(x, 0.0) - jnp.log1p(jnp.exp(-jnp.abs(x)))


def _rms(x, g):
    return x * lax.rsqrt(jnp.mean(x * x, axis=-1, keepdims=True) + EPS) * g


def _dot(a, b):
    return jnp.dot(a, b, preferred_element_type=F32)


def _dot_nt(a, b):
    return lax.dot_general(a, b, (((1,), (1,)), ((), ())), preferred_element_type=F32)


def _cast_kernel(w_ref, o_ref):
    o_ref[...] = w_ref[...].astype(o_ref.dtype)


def _cast_layer(w, layer, *, tr):
    _, r, c = w.shape
    return pl.pallas_call(
        _cast_kernel,
        grid=(r // tr,),
        in_specs=[pl.BlockSpec((None, tr, c), lambda i: (layer, i, 0))],
        out_specs=pl.BlockSpec((tr, c), lambda i: (i, 0)),
        out_shape=jax.ShapeDtypeStruct((r, c), BF16),
        compiler_params=_params("parallel"),
        name="cast_weight",
    )(w)


def _cast_w_in_kernel(wt_ref, gt_ref, main_ref, gate_ref, *, n_gates):
    main_ref[...] = wt_ref[0].T.astype(BF16)

    @pl.when(pl.program_id(0) == 0)
    def _():
        gate_ref[...] = jnp.zeros_like(gate_ref)
        gate_ref[:, :n_gates] = gt_ref[0].T.astype(BF16)


def _cast_w_in(w_in, layer, *, gate_lo, n_gates, tc):
    wt = jnp.swapaxes(w_in, 1, 2)
    _, c, d = wt.shape
    n_main = c - n_gates

    def src_rows(i):
        return pl.multiple_of(jnp.where(i * tc >= gate_lo, i * tc + n_gates, i * tc), n_gates)

    return pl.pallas_call(
        functools.partial(_cast_w_in_kernel, n_gates=n_gates),
        grid=(n_main // tc,),
        in_specs=[
            pl.BlockSpec((pl.Element(1), pl.Element(tc), pl.Element(d)), lambda i: (layer, src_rows(i), 0)),
            pl.BlockSpec((pl.Element(1), pl.Element(n_gates), pl.Element(d)), lambda i: (layer, gate_lo, 0)),
        ],
        out_specs=[pl.BlockSpec((d, tc), lambda i: (0, i)), pl.BlockSpec((d, LANES), lambda i: (0, 0))],
        out_shape=[jax.ShapeDtypeStruct((d, n_main), BF16), jax.ShapeDtypeStruct((d, LANES), BF16)],
        compiler_params=_params("arbitrary"),
        name="cast_w_in",
    )(wt, wt)


_FFN_ROWS = 256


def _ffn_kernel(x_ref, g_ref, wg_ref, wu_ref, wd_ref, fg_ref, *rest, final_norm, n_cast):
    cast_in, (o_ref, *cast_out), xn_ref = rest[:n_cast], rest[n_cast:2 * n_cast + 1], rest[2 * n_cast + 1]
    j = pl.program_id(1)
    last = pl.num_programs(1) - 1
    tm, d = o_ref.shape

    def step(first, final):
        if first:
            for r in range(0, tm, _FFN_ROWS):
                xn_ref[r:r + _FFN_ROWS, :] = _rms(x_ref[r:r + _FFN_ROWS, :], g_ref[...]).astype(BF16)
        xn = xn_ref[...]
        gate = _dot(xn, wg_ref[...])
        up = _dot(xn, wu_ref[...])
        act = (gate * _sigmoid(gate) * up).astype(BF16)
        down = _dot(act, wd_ref[...])
        if first:
            o_ref[...] = down
        elif final and not final_norm:
            o_ref[...] = x_ref[...] + 0.5 * (o_ref[...] + down)
        else:
            o_ref[...] += down
        for src, dst in zip(cast_in, cast_out):
            dst[...] = src[...].astype(dst.dtype)

    pl.when(j == 0)(lambda: step(True, False))
    pl.when(jnp.logical_and(j > 0, j < last))(lambda: step(False, False))
    pl.when(j == last)(lambda: step(False, True))

    if final_norm:
        @pl.when(j == last)
        def _():
            def finish_rows(c, carry):
                rows = pl.ds(pl.multiple_of(c * _FFN_ROWS, _FFN_ROWS), _FFN_ROWS)
                o_ref[rows, :] = _rms(x_ref[rows, :] + 0.5 * o_ref[rows, :], fg_ref[...])
                return carry
            lax.fori_loop(0, tm // _FFN_ROWS, finish_rows, 0)


def _cast_plan(w, n_i, n_j):
    _, r, c = w.shape
    steps = n_i * n_j
    if r % n_i == 0 and c % n_j == 0 and (c // n_j) % LANES == 0 and (r // n_i) % 16 == 0:
        return (r // n_i, c // n_j), (lambda i, j: (i, j))
    assert r % steps == 0 and (r // steps) % 16 == 0, (w.shape, n_i, n_j)
    return (r // steps, c), (lambda i, j: (i * n_j + j, 0))


def _ffn(h, g, w_gu, w_down, final_g, *, final_norm, tm, tf, cast=()):
    n, d = h.shape
    f = w_down.shape[0]
    nf = f // tf
    grid = (n // tm, nf)
    cast_specs_in, cast_specs_out, cast_shapes = [], [], []
    for w, layer in cast:
        blk, imap = _cast_plan(w, *grid)
        cast_specs_in.append(pl.BlockSpec((None,) + blk, functools.partial(lambda i, j, m, l: (l,) + m(i, j), m=imap, l=layer)))
        cast_specs_out.append(pl.BlockSpec(blk, imap))
        cast_shapes.append(jax.ShapeDtypeStruct(w.shape[1:], BF16))
    outs = pl.pallas_call(
        functools.partial(_ffn_kernel, final_norm=final_norm, n_cast=len(cast)),
        grid=grid,
        in_specs=[
            pl.BlockSpec((tm, d), lambda i, j: (i, 0)),
            pl.BlockSpec((1, d), lambda i, j: (0, 0)),
            pl.BlockSpec((d, tf), lambda i, j: (0, j)),
            pl.BlockSpec((d, tf), lambda i, j: (0, j + nf)),
            pl.BlockSpec((tf, d), lambda i, j: (j, 0)),
            pl.BlockSpec((1, d), lambda i, j: (0, 0)),
        ] + cast_specs_in,
        out_specs=[pl.BlockSpec((tm, d), lambda i, j: (i, 0))] + cast_specs_out,
        out_shape=[jax.ShapeDtypeStruct((n, d), F32)] + cast_shapes,
        scratch_shapes=[pltpu.VMEM((tm, d), BF16)],
        compiler_params=_params("parallel", "arbitrary", vmem=FFN_VMEM_LIMIT_BYTES),
        name="ffn",
    )(h, g.reshape(1, d), w_gu, w_gu, w_down, final_g.reshape(1, d), *[w for w, _ in cast])
    return outs[0], outs[1:]


def _mix_in_kernel(x_ref, g_ref, w_ref, wgate_ref, wo_ref, o_ref, og_ref, wob_ref, xn_ref):
    def step(first):
        if first:
            rows = _FFN_ROWS
            for r in range(0, xn_ref.shape[0], rows):
                xn_ref[r:r + rows, :] = _rms(x_ref[r:r + rows, :], g_ref[...]).astype(BF16)
            og_ref[...] = _dot(xn_ref[...], wgate_ref[...])
        res = _dot(xn_ref[...], w_ref[...])
        for k in range(o_ref.shape[0]):
            o_ref[k] = res[:, k * HEAD_DIM:(k + 1) * HEAD_DIM]
        wob_ref[...] = wo_ref[...].astype(BF16)

    j = pl.program_id(1)
    pl.when(j == 0)(lambda: step(True))
    pl.when(j > 0)(lambda: step(False))


def _mix_in(h, g, w_main, w_gate, w_out, layer, *, tm, tn):
    n, d = h.shape
    c = w_main.shape[1]
    nb = tn // HEAD_DIM
    grid = (n // tm, c // tn)
    wo_blk, wo_map = _cast_plan(w_out, *grid)
    return pl.pallas_call(
        _mix_in_kernel,
        grid=grid,
        in_specs=[
            pl.BlockSpec((tm, d), lambda i, j: (i, 0)),
            pl.BlockSpec((1, d), lambda i, j: (0, 0)),
            pl.BlockSpec((d, tn), lambda i, j: (0, j)),
            pl.BlockSpec((d, LANES), lambda i, j: (0, 0)),
            pl.BlockSpec((None,) + wo_blk, lambda i, j: (layer,) + wo_map(i, j)),
        ],
        out_specs=[
            pl.BlockSpec((nb, tm, HEAD_DIM), lambda i, j: (j, i, 0)),
            pl.BlockSpec((tm, LANES), lambda i, j: (i, 0)),
            pl.BlockSpec(wo_blk, wo_map),
        ],
        out_shape=[jax.ShapeDtypeStruct((c // HEAD_DIM, n, HEAD_DIM), F32), jax.ShapeDtypeStruct((n, LANES), F32),
                   jax.ShapeDtypeStruct(w_out.shape[1:], BF16)],
        scratch_shapes=[pltpu.VMEM((tm, d), BF16)],
        compiler_params=_params("parallel", "arbitrary"),
        name="mix_in",
    )(h, g.reshape(1, d), w_main, w_gate, w_out)


def _mix_out_kernel(h_ref, m_ref, na_ref, nb_ref, r_ref, wm_ref, wna_ref, wnb_ref, wr_ref, o_ref):
    acc = (_dot(m_ref[...], wm_ref[...]) + _dot(na_ref[...], wna_ref[...])
           + _dot(nb_ref[...], wnb_ref[...]) + _dot(r_ref[...], wr_ref[...]))
    o_ref[...] = h_ref[...] + acc


def _mix_out(h, m_out, n_out, r_out, w_out, *, tm, tn):
    n, d = h.shape
    kb = m_out.shape[1]
    assert n_out.shape[1] == 2 * kb and r_out.shape[1] == kb and w_out.shape[0] == 4 * kb
    act = lambda c: pl.BlockSpec((tm, kb), lambda i, j: (i, c))
    wblk = lambda r: pl.BlockSpec((kb, tn), lambda i, j: (r, j))
    return pl.pallas_call(
        _mix_out_kernel,
        grid=(n // tm, d // tn),
        in_specs=[pl.BlockSpec((tm, tn), lambda i, j: (i, j)), act(0), act(0), act(1), act(0),
                  wblk(0), wblk(1), wblk(2), wblk(3)],
        out_specs=pl.BlockSpec((tm, tn), lambda i, j: (i, j)),
        out_shape=jax.ShapeDtypeStruct((n, d), F32),
        compiler_params=_params("parallel", "arbitrary"),
        name="mix_out",
    )(h, m_out, n_out, n_out, r_out, w_out, w_out, w_out, w_out)


def _mlstm_step(qb, kb, kt, va, b, u_row, mask, g, st_ref, m_st):
    dmat = jnp.where(mask, b + u_row, NEG)
    inter = b + m_st
    m_t = jnp.maximum(inter, jnp.max(dmat, axis=-1, keepdims=True))
    s = _dot_nt(qb, kb) * jnp.exp(dmat - m_t)
    st = st_ref[...]
    r_intra = _dot(s.astype(BF16), va)
    r_inter = _dot(qb, st.astype(BF16))
    w_inter = jnp.exp(inter - m_t)
    num = r_intra[:, :HEAD_DIM] + w_inter * r_inter[:, :HEAD_DIM]
    den = r_intra[:, HEAD_DIM:] + w_inter * r_inter[:, HEAD_DIM:]
    h_out = num / jnp.maximum(jnp.abs(den), jnp.exp(-m_t))
    a = g + u_row
    m_new = jnp.maximum(g + m_st, jnp.max(a, axis=-1, keepdims=True))
    ktw = (kt * jnp.exp(a - m_new)).astype(BF16)
    st_ref[...] = jnp.exp(g + m_st - m_new) * st + _dot(ktw, va)
    return h_out, m_new


_MLSTM_HEADS = 2


def _mlstm_kernel(q_ref, k_ref, v_ref, o_ref, gt_ref, cwq_ref, cwk_ref, cbq_ref, cbk_ref, gb_ref, gain_ref,
                  out_ref, qs_ref, ks_ref, kt_ref, va_ref, bp_ref, bs_ref, ut_ref, bf_ref, bb_ref, hf_ref, hb_ref,
                  sf_ref, sb_ref):
    t_len = q_ref.shape[1]
    n_chunks = t_len // CHUNK
    row = lax.broadcasted_iota(jnp.int32, (t_len, 1), 0)
    lane = lax.broadcasted_iota(jnp.int32, (1, LANES), 1)

    @pl.when(pl.program_id(1) == 0)
    def _():
        gates = gt_ref[...] + gb_ref[...]
        logf = _log_sigmoid(gates)
        rin = row & (CHUNK - 1)
        bp = logf
        bs = logf
        k = 1
        while k < CHUNK:
            bp = bp + jnp.where(rin >= k, pltpu.roll(bp, k, 0), 0.0)
            bs = bs + jnp.where(rin < CHUNK - k, pltpu.roll(bs, t_len - k, 0), 0.0)
            k *= 2
        bp_ref[...] = bp
        bs_ref[...] = bs
        u = gates - jnp.where(lane < 2 * H_MLSTM, pltpu.roll(bp, LANES - H_MLSTM, 1),
                              pltpu.roll(bs, LANES - H_MLSTM, 1))
        for c in range(n_chunks):
            ut_ref[:, c * CHUNK:(c + 1) * CHUNK] = u[c * CHUNK:(c + 1) * CHUNK, :].T[0:ut_ref.shape[0], :]

    def conv_silu(x, w, b):
        prev = jnp.where(row >= 1, pltpu.roll(x, 1, 0), 0.0)
        nxt = jnp.where(row <= t_len - 2, pltpu.roll(x, t_len - 1, 0), 0.0)
        y = prev * w[0:1, :] + x * w[1:2, :] + nxt * w[2:3, :] + b
        return y * _sigmoid(y)

    def lane_bcast(x, idx):
        return jnp.broadcast_to(jnp.sum(jnp.where(lane == idx, x, 0.0), axis=-1, keepdims=True), x.shape)

    ti = lax.broadcasted_iota(jnp.int32, (CHUNK, CHUNK), 0)
    si = lax.broadcasted_iota(jnp.int32, (CHUNK, CHUNK), 1)
    mask_f = ti >= si
    mask_b = ti <= si
    sub = lax.broadcasted_iota(jnp.int32, (2 * H_MLSTM, 1), 0)

    for hh in range(_MLSTM_HEADS):
        head = pl.program_id(1) * _MLSTM_HEADS + hh
        hs = slice(hh * HEAD_DIM, (hh + 1) * HEAD_DIM)

        def rowsel(x, head=head):
            return jnp.sum(jnp.where(sub == head, x, 0.0), axis=0, keepdims=True)

        qs_ref[hh] = (conv_silu(q_ref[hh], cwq_ref[:, hs], cbq_ref[:, hs]) * (HEAD_DIM ** -0.5)).astype(BF16)
        k = conv_silu(k_ref[hh], cwk_ref[:, hs], cbk_ref[:, hs])
        ks_ref[hh] = k.astype(BF16)
        for c in range(n_chunks):
            kt_ref[hh, :, c * CHUNK:(c + 1) * CHUNK] = k[c * CHUNK:(c + 1) * CHUNK, :].T
        va_ref[hh, :, :HEAD_DIM] = v_ref[hh].astype(BF16)
        va_ref[hh, :, HEAD_DIM:] = jnp.ones((t_len, HEAD_DIM), BF16)
        bf_ref[hh] = lane_bcast(bp_ref[...], H_MLSTM + head)
        bb_ref[hh] = lane_bcast(bs_ref[...], 3 * H_MLSTM + head)
        sf_ref[hh] = jnp.zeros(sf_ref.shape[1:], F32)
        sb_ref[hh] = jnp.zeros(sb_ref.shape[1:], F32)

        mf = mb = jnp.full((1, 1), NEG, F32)
        for i in range(n_chunks):
            sf = slice(i * CHUNK, (i + 1) * CHUNK)
            sb = slice((n_chunks - 1 - i) * CHUNK, (n_chunks - i) * CHUNK)

            b = bf_ref[hh, sf, :]
            hf, mf = _mlstm_step(qs_ref[hh, sf, :], ks_ref[hh, sf, :], kt_ref[hh, :, sf], va_ref[hh, sf, :], b,
                                 rowsel(ut_ref[0:2 * H_MLSTM, sf]), mask_f, b[CHUNK - 1:CHUNK, 0:1],
                                 sf_ref.at[hh], mf)
            hf_ref[hh, sf, :] = hf

            b = bb_ref[hh, sb, :]
            hb, mb = _mlstm_step(qs_ref[hh, sb, :], ks_ref[hh, sb, :], kt_ref[hh, :, sb], va_ref[hh, sb, :], b,
                                 rowsel(ut_ref[2 * H_MLSTM:4 * H_MLSTM, sb]), mask_b, b[0:1, 0:1],
                                 sb_ref.at[hh], mb)
            hb_ref[hh, sb, :] = hb

        y = _sigmoid(o_ref[hh]) * (hf_ref[hh] + hb_ref[hh])
        out_ref[:, hs] = _rms(y, gain_ref[:, hs]).astype(out_ref.dtype)


def _mlstm(proj, gates, conv_w, conv_b, gate_b, gain, *, batch, t_len):
    nh = _MLSTM_HEADS
    w = H_MLSTM * HEAD_DIM
    assert H_MLSTM % nh == 0 and all(b % nh == 0 for b in (_BLK_MQ, _BLK_MK, _BLK_MV, _BLK_MO))
    blk = lambda off: pl.BlockSpec((nh, t_len, HEAD_DIM), lambda b, h: (off // nh + h, b, 0))
    vec = lambda off: pl.BlockSpec((1, nh * HEAD_DIM), lambda b, h: (0, off // nh + h))
    seq = pltpu.VMEM((nh, t_len, HEAD_DIM), F32)
    seqb = pltpu.VMEM((nh, t_len, HEAD_DIM), BF16)
    shared = pltpu.VMEM((t_len, HEAD_DIM), F32)
    state = pltpu.VMEM((nh, HEAD_DIM, 2 * HEAD_DIM), F32)
    return pl.pallas_call(
        _mlstm_kernel,
        grid=(batch, H_MLSTM // nh),
        in_specs=[
            blk(_BLK_MQ), blk(_BLK_MK), blk(_BLK_MV), blk(_BLK_MO),
            pl.BlockSpec((t_len, LANES), lambda b, h: (b, 0)),
            pl.BlockSpec((3, nh * HEAD_DIM), lambda b, h: (0, h)),
            pl.BlockSpec((3, nh * HEAD_DIM), lambda b, h: (0, H_MLSTM // nh + h)),
            vec(0), vec(H_MLSTM),
            pl.BlockSpec((1, LANES), lambda b, h: (0, 0)),
            vec(0),
        ],
        out_specs=pl.BlockSpec((t_len, nh * HEAD_DIM), lambda b, h: (b, h)),
        out_shape=jax.ShapeDtypeStruct((batch * t_len, w), BF16),
        scratch_shapes=[
            seqb, seqb, pltpu.VMEM((nh, HEAD_DIM, t_len), F32), pltpu.VMEM((nh, t_len, 2 * HEAD_DIM), BF16),
            shared, shared, pltpu.VMEM((4 * H_MLSTM, t_len), F32), seq, seq, seq, seq, state, state,
        ],
        compiler_params=_params("parallel", "arbitrary"),
        name="mlstm",
    )(proj, proj, proj, proj, gates, conv_w, conv_w, conv_b.reshape(1, -1), conv_b.reshape(1, -1),
      gate_b, gain.reshape(1, -1))


_NA_HEADS = 2


def _na_kernel(q_ref, k_ref, v_ref, bias_ref, out_ref, qb_ref, kb_ref, va_ref):
    t_len = q_ref.shape[1]
    rows = t_len // GRID_W
    wr = min(WIN_R, rows)
    n_keys = wr * GRID_W

    for hh in range(_NA_HEADS):
        qb_ref[hh] = (q_ref[hh] * (HEAD_DIM ** -0.5)).astype(BF16)
        kb_ref[hh] = k_ref[hh].astype(BF16)
        va_ref[hh, :, :HEAD_DIM] = v_ref[hh].astype(BF16)
        va_ref[hh, :, HEAD_DIM:] = jnp.ones((t_len, HEAD_DIM), BF16)

    for r in range(rows):
        r0 = min(max(r - wr // 2, 0), rows - wr)
        sq = slice(r * GRID_W, (r + 1) * GRID_W)
        sk = slice(r0 * GRID_W, r0 * GRID_W + n_keys)
        pairs = [WIN_R - 1 + r0 + 2 * c - r for c in range(wr // 2)]
        for hh in range(_NA_HEADS):
            bias = jnp.concatenate([bias_ref[hh, a] for a in pairs], axis=1)
            s = _dot_nt(qb_ref[hh, sq, :], kb_ref[hh, sk, :]) + bias
            e = jnp.exp(s - jnp.max(s, axis=-1, keepdims=True))
            o = _dot(e.astype(BF16), va_ref[hh, sk, :])
            out_ref[sq, hh * HEAD_DIM:(hh + 1) * HEAD_DIM] = (o[:, :HEAD_DIM] / o[:, HEAD_DIM:]).astype(out_ref.dtype)


def _na_bias_pairs(rpb):
    qc = np.arange(GRID_W)[:, None]
    kc = np.arange(GRID_W)[None, :]
    win_c0 = np.clip(qc - WIN_C // 2, 0, GRID_W - WIN_C)
    ok = (kc >= win_c0) & (kc < win_c0 + WIN_C)
    sel_c = np.arange(2 * WIN_C - 1) == (kc - qc + WIN_C - 1)[..., None]
    t = jnp.einsum("hrc,qjc->hrqj", rpb.astype(F32), jnp.asarray(sel_c, F32), precision=lax.Precision.HIGHEST)
    t = jnp.where(jnp.asarray(ok), t, NEG)
    return jnp.concatenate([t[:, :-1], t[:, 1:]], axis=-1)


def _na(proj, bias_tab, *, batch, t_len):
    assert min(WIN_R, t_len // GRID_W) % 2 == 0 and 2 * GRID_W == LANES
    assert H_NA % _NA_HEADS == 0 and _BLK_NQ % _NA_HEADS == 0 and _BLK_NK % _NA_HEADS == 0 and _BLK_NV % _NA_HEADS == 0
    blk = lambda off: pl.BlockSpec((_NA_HEADS, t_len, HEAD_DIM), lambda b, h: (off // _NA_HEADS + h, b, 0))
    seq = pltpu.VMEM((_NA_HEADS, t_len, HEAD_DIM), BF16)
    return pl.pallas_call(
        _na_kernel,
        grid=(batch, H_NA // _NA_HEADS),
        in_specs=[
            blk(_BLK_NQ), blk(_BLK_NK), blk(_BLK_NV),
            pl.BlockSpec((_NA_HEADS,) + bias_tab.shape[1:], lambda b, h: (h, 0, 0, 0)),
        ],
        out_specs=pl.BlockSpec((t_len, _NA_HEADS * HEAD_DIM), lambda b, h: (b, h)),
        out_shape=jax.ShapeDtypeStruct((batch * t_len, H_NA * HEAD_DIM), BF16),
        scratch_shapes=[seq, seq, pltpu.VMEM((_NA_HEADS, t_len, 2 * HEAD_DIM), BF16)],
        compiler_params=_params("parallel", "parallel"),
        name="natten",
    )(proj, proj, proj, bias_tab)


_RET_HEADS = 2


def _ret_kernel(q_ref, k_ref, v_ref, g_ref, cos_ref, sin_ref, dl_ref, gain_ref, out_ref,
                qb_ref, qx_ref, kb_ref, kz_ref, vb_ref, o_ref, kv_ref, r_ref):
    t_len = q_ref.shape[1]
    n_chunks = t_len // CHUNK

    def rope(x):
        return x * cos_ref[...] + pltpu.roll(x, HEAD_DIM // 2, 1) * sin_ref[...]

    def chunk(c):
        return slice(c * CHUNK, (c + 1) * CHUNK)

    diff = (lax.broadcasted_iota(jnp.int32, (CHUNK, CHUNK), 0)
            - lax.broadcasted_iota(jnp.int32, (CHUNK, CHUNK), 1)).astype(F32)
    pos = lax.broadcasted_iota(jnp.int32, (CHUNK, HEAD_DIM), 0).astype(F32)

    for hh in range(_RET_HEADS):
        lg = _log_sigmoid(dl_ref[hh])
        lg_f, lg_b = lg[0:1, :], lg[1:2, :]
        intra = (jnp.where(diff >= 0, jnp.exp(jnp.where(diff >= 0, diff, 0.0) * lg_f), 0.0)
                 + jnp.where(diff <= 0, jnp.exp(jnp.where(diff <= 0, -diff, 0.0) * lg_b), 0.0))
        gch_f = jnp.exp(CHUNK * lg_f)
        gch_b = jnp.exp(CHUNK * lg_b)
        xi_f = jnp.exp((pos + 1.0) * lg_f)
        xi_b = jnp.exp((CHUNK - pos) * lg_b)
        zeta_f = jnp.exp((CHUNK - 1.0 - pos) * lg_f)
        zeta_b = jnp.exp(pos * lg_b)
        q = rope(q_ref[hh])
        k = rope(k_ref[hh]) * (HEAD_DIM ** -0.5)
        qb_ref[hh] = q.astype(BF16)
        kb_ref[hh] = k.astype(BF16)
        vb_ref[hh] = v_ref[hh].astype(BF16)
        for c in range(n_chunks):
            cs = chunk(c)
            qx_ref[hh, cs, :HEAD_DIM] = (q[cs, :] * xi_f).astype(BF16)
            qx_ref[hh, cs, HEAD_DIM:] = (q[cs, :] * xi_b).astype(BF16)
            kz_ref[hh, :HEAD_DIM, cs] = (k[cs, :] * zeta_f).T.astype(BF16)
            kz_ref[hh, HEAD_DIM:, cs] = (k[cs, :] * zeta_b).T.astype(BF16)

        for c in range(n_chunks):
            sl = chunk(c)
            vb = vb_ref[hh, sl, :]
            s = _dot_nt(qb_ref[hh, sl, :], kb_ref[hh, sl, :]) * intra
            o_ref[hh, sl, :] = _dot(s.astype(BF16), vb)
            kv_ref[hh, c] = _dot(kz_ref[hh, :, sl], vb)

        def state_pass(i, carry, hh=hh, gch_f=gch_f, gch_b=gch_b):
            rf, rb = carry
            cb = n_chunks - 1 - i
            r_ref[hh, i, :HEAD_DIM, :] = rf.astype(BF16)
            r_ref[hh, cb, HEAD_DIM:, :] = rb.astype(BF16)
            return gch_f * rf + kv_ref[hh, i, :HEAD_DIM, :], gch_b * rb + kv_ref[hh, cb, HEAD_DIM:, :]

        zero = jnp.zeros((HEAD_DIM, HEAD_DIM), F32)
        lax.fori_loop(0, n_chunks, state_pass, (zero, zero))

        for c in range(n_chunks):
            o_ref[hh, chunk(c), :] += _dot(qx_ref[hh, chunk(c), :], r_ref[hh, c])

        g = g_ref[hh]
        hs = slice(hh * HEAD_DIM, (hh + 1) * HEAD_DIM)
        out_ref[:, hs] = (g * _sigmoid(g) * _rms(o_ref[hh], gain_ref[:, hs])).astype(out_ref.dtype)


def _ret(proj, cos2, sin2, decay_rows, gain, *, batch, t_len):
    n_chunks = t_len // CHUNK
    nh = _RET_HEADS
    assert H_RET % nh == 0 and all(b % nh == 0 for b in (_BLK_RQ, _BLK_RK, _BLK_RV, _BLK_RG))
    blk = lambda off: pl.BlockSpec((nh, t_len, HEAD_DIM), lambda b, h: (off // nh + h, b, 0))
    tab = pl.BlockSpec((t_len, HEAD_DIM), lambda b, h: (0, 0))
    seqb = pltpu.VMEM((nh, t_len, HEAD_DIM), BF16)
    return pl.pallas_call(
        _ret_kernel,
        grid=(batch, H_RET // nh),
        in_specs=[
            blk(_BLK_RQ), blk(_BLK_RK), blk(_BLK_RV), blk(_BLK_RG), tab, tab,
            pl.BlockSpec((nh, 2, LANES), lambda b, h: (h, 0, 0)),
            pl.BlockSpec((1, nh * HEAD_DIM), lambda b, h: (0, h)),
        ],
        out_specs=pl.BlockSpec((t_len, nh * HEAD_DIM), lambda b, h: (b, h)),
        out_shape=jax.ShapeDtypeStruct((batch * t_len, H_RET * HEAD_DIM), BF16),
        scratch_shapes=[
            seqb, pltpu.VMEM((nh, t_len, 2 * HEAD_DIM), BF16), seqb, pltpu.VMEM((nh, 2 * HEAD_DIM, t_len), BF16), seqb,
            pltpu.VMEM((nh, t_len, HEAD_DIM), F32),
            pltpu.VMEM((nh, n_chunks, 2 * HEAD_DIM, HEAD_DIM), F32),
            pltpu.VMEM((nh, n_chunks, 2 * HEAD_DIM, HEAD_DIM), BF16),
        ],
        compiler_params=_params("parallel", "parallel"),
        name="retention",
    )(proj, proj, proj, proj, cos2, sin2, decay_rows, gain.reshape(1, -1))


def _rope_tables(t_len):
    inv = ROPE_BASE ** (-jnp.arange(0, HEAD_DIM, 2, dtype=F32) / HEAD_DIM)
    ang = jnp.arange(t_len, dtype=F32)[:, None] * inv[None, :]
    cos, sin = jnp.cos(ang), jnp.sin(ang)
    return jnp.concatenate([cos, cos], axis=-1), jnp.concatenate([-sin, sin], axis=-1)


def _pick(n, pref):
    return pref if n % pref == 0 else n


def kernel(x, ffn1_norm, ffn1_w_gu, ffn1_w_down, mix_norm, w_in, mlstm_conv_w, mlstm_conv_b, mlstm_gate_b,
           mlstm_head_norm, na_rpb, ret_decay_logit, ret_head_norm, w_out, ffn2_norm, ffn2_w_gu, ffn2_w_down,
           final_norm):
    batch, t_len, d = x.shape
    n = batch * t_len
    depth = ffn1_norm.shape[0]
    n_gates = 4 * H_MLSTM
    gate_lo = 4 * H_MLSTM * HEAD_DIM

    tm = _pick(n, 1024)
    ffn = functools.partial(_ffn, tm=tm, tf=512)
    cos2, sin2 = _rope_tables(t_len)

    h = x.reshape(n, d)
    w_gu1, w_down1 = _cast_layer(ffn1_w_gu, 0, tr=256), _cast_layer(ffn1_w_down, 0, tr=704)
    for l in range(depth):
        h, (w_gu2, w_down2) = ffn(h, ffn1_norm[l], w_gu1, w_down1, final_norm, final_norm=False,
                                  cast=((ffn2_w_gu, l), (ffn2_w_down, l)))

        w_main, w_gate = _cast_w_in(w_in, l, gate_lo=gate_lo, n_gates=n_gates, tc=512)
        proj, gates, wo = _mix_in(h, mix_norm[l], w_main, w_gate, w_out, l, tm=tm, tn=1792)

        gate_b = jnp.pad(mlstm_gate_b[l].astype(F32).reshape(1, n_gates), ((0, 0), (0, LANES - n_gates)))
        m_out = _mlstm(proj, gates, mlstm_conv_w[l], mlstm_conv_b[l], gate_b, mlstm_head_norm[l],
                       batch=batch, t_len=t_len)
        n_out = _na(proj, _na_bias_pairs(na_rpb[l]), batch=batch, t_len=t_len)
        decay_rows = jnp.broadcast_to(ret_decay_logit[l].astype(F32).T[:, :, None], (H_RET, 2, LANES))
        r_out = _ret(proj, cos2, sin2, decay_rows, ret_head_norm[l], batch=batch, t_len=t_len)

        h = _mix_out(h, m_out, n_out, r_out, wo, tm=_pick(n, 512), tn=d)

        last = l == depth - 1
        h, nxt = ffn(h, ffn2_norm[l], w_gu2, w_down2, final_norm, final_norm=last,
                     cast=() if last else ((ffn1_w_gu, l + 1), (ffn1_w_down, l + 1)))
        if not last:
            w_gu1, w_down1 = nxt
    return h.reshape(batch, t_len, d)
```

```python
import functools

import numpy as np
import jax
import jax.numpy as jnp
from jax import lax
from jax.experimental import pallas as pl
from jax.experimental.pallas import tpu as pltpu

F32 = jnp.float32
BF16 = jnp.bfloat16

HEAD_DIM = 128
H_MLSTM = 4
H_NA = 8
H_RET = 4
CHUNK = 128
GRID_W = 64
WIN_R = 8
WIN_C = 16
ROPE_BASE = 10000.0
EPS = 1e-6
NEG = -1e30

LANES = 128
VMEM_BYTES = 64 * 1024 * 1024
VMEM_LIMIT_BYTES = VMEM_BYTES - 8 * 1024 * 1024
FFN_VMEM_LIMIT_BYTES = VMEM_BYTES - 3 * 1024 * 1024

_BLK_MQ, _BLK_MK, _BLK_MV, _BLK_MO = 0, 4, 8, 12
_BLK_NQ, _BLK_NK, _BLK_NV = 16, 24, 32
_BLK_RQ, _BLK_RK, _BLK_RV, _BLK_RG = 40, 44, 48, 52


def _params(*sem, vmem=VMEM_LIMIT_BYTES):
    return pltpu.CompilerParams(dimension_semantics=sem, vmem_limit_bytes=vmem)


def _sigmoid(x):
    return 1.0 / (1.0 + jnp.exp(-x))


def _log_sigmoid(x):
    return jnp.minimum(x, 0.0) - jnp.log1p(jnp.exp(-jnp.abs(x)))


def _rms(x, g):
    return x * lax.rsqrt(jnp.mean(x * x, axis=-1, keepdims=True) + EPS) * g


def _dot(a, b):
    return jnp.dot(a, b, preferred_element_type=F32)


def _dot_nt(a, b):
    return lax.dot_general(a, b, (((1,), (1,)), ((), ())), preferred_element_type=F32)


def _cast_kernel(w_ref, o_ref):
    o_ref[...] = w_ref[...].astype(o_ref.dtype)


def _cast_layer(w, layer, *, tr):
    _, r, c = w.shape
    return pl.pallas_call(
        _cast_kernel,
        grid=(r // tr,),
        in_specs=[pl.BlockSpec((None, tr, c), lambda i: (layer, i, 0))],
        out_specs=pl.BlockSpec((tr, c), lambda i: (i, 0)),
        out_shape=jax.ShapeDtypeStruct((r, c), BF16),
        compiler_params=_params("parallel"),
        name="cast_weight",
    )(w)


def _cast_w_in_kernel(wt_ref, gt_ref, main_ref, gate_ref, *, n_gates):
    main_ref[...] = wt_ref[0].T.astype(BF16)

    @pl.when(pl.program_id(0) == 0)
    def _():
        gate_ref[...] = jnp.zeros_like(gate_ref)
        gate_ref[:, :n_gates] = gt_ref[0].T.astype(BF16)


def _cast_w_in(w_in, layer, *, gate_lo, n_gates, tc):
    wt = jnp.swapaxes(w_in, 1, 2)
    _, c, d = wt.shape
    n_main = c - n_gates

    def src_rows(i):
        return pl.multiple_of(jnp.where(i * tc >= gate_lo, i * tc + n_gates, i * tc), n_gates)

    return pl.pallas_call(
        functools.partial(_cast_w_in_kernel, n_gates=n_gates),
        grid=(n_main // tc,),
        in_specs=[
            pl.BlockSpec((pl.Element(1), pl.Element(tc), pl.Element(d)), lambda i: (layer, src_rows(i), 0)),
            pl.BlockSpec((pl.Element(1), pl.Element(n_gates), pl.Element(d)), lambda i: (layer, gate_lo, 0)),
        ],
        out_specs=[pl.BlockSpec((d, tc), lambda i: (0, i)), pl.BlockSpec((d, LANES), lambda i: (0, 0))],
        out_shape=[jax.ShapeDtypeStruct((d, n_main), BF16), jax.ShapeDtypeStruct((d, LANES), BF16)],
        compiler_params=_params("arbitrary"),
        name="cast_w_in",
    )(wt, wt)


_FFN_ROWS = 256


def _ffn_kernel(x_ref, g_ref, wg_ref, wu_ref, wd_ref, fg_ref, *rest, final_norm, n_cast):
    cast_in, (o_ref, *cast_out), xn_ref = rest[:n_cast], rest[n_cast:2 * n_cast + 1], rest[2 * n_cast + 1]
    j = pl.program_id(1)
    last = pl.num_programs(1) - 1
    tm, d = o_ref.shape

    def step(first, final):
        if first:
            for r in range(0, tm, _FFN_ROWS):
                xn_ref[r:r + _FFN_ROWS, :] = _rms(x_ref[r:r + _FFN_ROWS, :], g_ref[...]).astype(BF16)
        xn = xn_ref[...]
        gate = _dot(xn, wg_ref[...])
        up = _dot(xn, wu_ref[...])
        act = (gate * _sigmoid(gate) * up).astype(BF16)
        down = _dot(act, wd_ref[...])
        if first:
            o_ref[...] = down
        elif final and not final_norm:
            o_ref[...] = x_ref[...] + 0.5 * (o_ref[...] + down)
        else:
            o_ref[...] += down
        for src, dst in zip(cast_in, cast_out):
            dst[...] = src[...].astype(dst.dtype)

    pl.when(j == 0)(lambda: step(True, False))
    pl.when(jnp.logical_and(j > 0, j < last))(lambda: step(False, False))
    pl.when(j == last)(lambda: step(False, True))

    if final_norm:
        @pl.when(j == last)
        def _():
            def finish_rows(c, carry):
                rows = pl.ds(pl.multiple_of(c * _FFN_ROWS, _FFN_ROWS), _FFN_ROWS)
                o_ref[rows, :] = _rms(x_ref[rows, :] + 0.5 * o_ref[rows, :], fg_ref[...])
                return carry
            lax.fori_loop(0, tm // _FFN_ROWS, finish_rows, 0)


def _cast_plan(w, n_i, n_j):
    _, r, c = w.shape
    steps = n_i * n_j
    if r % n_i == 0 and c % n_j == 0 and (c // n_j) % LANES == 0 and (r // n_i) % 16 == 0:
        return (r // n_i, c // n_j), (lambda i, j: (i, j))
    assert r % steps == 0 and (r // steps) % 16 == 0, (w.shape, n_i, n_j)
    return (r // steps, c), (lambda i, j: (i * n_j + j, 0))


def _ffn(h, g, w_gu, w_down, final_g, *, final_norm, tm, tf, cast=()):
    n, d = h.shape
    f = w_down.shape[0]
    nf = f // tf
    grid = (n // tm, nf)
    cast_specs_in, cast_specs_out, cast_shapes = [], [], []
    for w, layer in cast:
        blk, imap = _cast_plan(w, *grid)
        cast_specs_in.append(pl.BlockSpec((None,) + blk, functools.partial(lambda i, j, m, l: (l,) + m(i, j), m=imap, l=layer)))
        cast_specs_out.append(pl.BlockSpec(blk, imap))
        cast_shapes.append(jax.ShapeDtypeStruct(w.shape[1:], BF16))
    outs = pl.pallas_call(
        functools.partial(_ffn_kernel, final_norm=final_norm, n_cast=len(cast)),
        grid=grid,
        in_specs=[
            pl.BlockSpec((tm, d), lambda i, j: (i, 0)),
            pl.BlockSpec((1, d), lambda i, j: (0, 0)),
            pl.BlockSpec((d, tf), lambda i, j: (0, j)),
            pl.BlockSpec((d, tf), lambda i, j: (0, j + nf)),
            pl.BlockSpec((tf, d), lambda i, j: (j, 0)),
            pl.BlockSpec((1, d), lambda i, j: (0, 0)),
        ] + cast_specs_in,
        out_specs=[pl.BlockSpec((tm, d), lambda i, j: (i, 0))] + cast_specs_out,
        out_shape=[jax.ShapeDtypeStruct((n, d), F32)] + cast_shapes,
        scratch_shapes=[pltpu.VMEM((tm, d), BF16)],
        compiler_params=_params("parallel", "arbitrary", vmem=FFN_VMEM_LIMIT_BYTES),
        name="ffn",
    )(h, g.reshape(1, d), w_gu, w_gu, w_down, final_g.reshape(1, d), *[w for w, _ in cast])
    return outs[0], outs[1:]


def _mix_in_kernel(x_ref, g_ref, w_ref, wgate_ref, wo_ref, o_ref, og_ref, wob_ref, xn_ref):
    def step(first):
        if first:
            rows = _FFN_ROWS
            for r in range(0, xn_ref.shape[0], rows):
                xn_ref[r:r + rows, :] = _rms(x_ref[r:r + rows, :], g_ref[...]).astype(BF16)
            og_ref[...] = _dot(xn_ref[...], wgate_ref[...])
        res = _dot(xn_ref[...], w_ref[...])
        for k in range(o_ref.shape[0]):
            o_ref[k] = res[:, k * HEAD_DIM:(k + 1) * HEAD_DIM]
        wob_ref[...] = wo_ref[...].astype(BF16)

    j = pl.program_id(1)
    pl.when(j == 0)(lambda: step(True))
    pl.when(j > 0)(lambda: step(False))


def _mix_in(h, g, w_main, w_gate, w_out, layer, *, tm, tn):
    n, d = h.shape
    c = w_main.shape[1]
    nb = tn // HEAD_DIM
    grid = (n // tm, c // tn)
    wo_blk, wo_map = _cast_plan(w_out, *grid)
    return pl.pallas_call(
        _mix_in_kernel,
        grid=grid,
        in_specs=[
            pl.BlockSpec((tm, d), lambda i, j: (i, 0)),
            pl.BlockSpec((1, d), lambda i, j: (0, 0)),
            pl.BlockSpec((d, tn), lambda i, j: (0, j)),
            pl.BlockSpec((d, LANES), lambda i, j: (0, 0)),
            pl.BlockSpec((None,) + wo_blk, lambda i, j: (layer,) + wo_map(i, j)),
        ],
        out_specs=[
            pl.BlockSpec((nb, tm, HEAD_DIM), lambda i, j: (j, i, 0)),
            pl.BlockSpec((tm, LANES), lambda i, j: (i, 0)),
            pl.BlockSpec(wo_blk, wo_map),
        ],
        out_shape=[jax.ShapeDtypeStruct((c // HEAD_DIM, n, HEAD_DIM), F32), jax.ShapeDtypeStruct((n, LANES), F32),
                   jax.ShapeDtypeStruct(w_out.shape[1:], BF16)],
        scratch_shapes=[pltpu.VMEM((tm, d), BF16)],
        compiler_params=_params("parallel", "arbitrary"),
        name="mix_in",
    )(h, g.reshape(1, d), w_main, w_gate, w_out)


def _mix_out_kernel(h_ref, m_ref, na_ref, nb_ref, r_ref, wm_ref, wna_ref, wnb_ref, wr_ref, o_ref):
    acc = (_dot(m_ref[...], wm_ref[...]) + _dot(na_ref[...], wna_ref[...])
           + _dot(nb_ref[...], wnb_ref[...]) + _dot(r_ref[...], wr_ref[...]))
    o_ref[...] = h_ref[...] + acc


def _mix_out(h, m_out, n_out, r_out, w_out, *, tm, tn):
    n, d = h.shape
    kb = m_out.shape[1]
    assert n_out.shape[1] == 2 * kb and r_out.shape[1] == kb and w_out.shape[0] == 4 * kb
    act = lambda c: pl.BlockSpec((tm, kb), lambda i, j: (i, c))
    wblk = lambda r: pl.BlockSpec((kb, tn), lambda i, j: (r, j))
    return pl.pallas_call(
        _mix_out_kernel,
        grid=(n // tm, d // tn),
        in_specs=[pl.BlockSpec((tm, tn), lambda i, j: (i, j)), act(0), act(0), act(1), act(0),
                  wblk(0), wblk(1), wblk(2), wblk(3)],
        out_specs=pl.BlockSpec((tm, tn), lambda i, j: (i, j)),
        out_shape=jax.ShapeDtypeStruct((n, d), F32),
        compiler_params=_params("parallel", "arbitrary"),
        name="mix_out",
    )(h, m_out, n_out, n_out, r_out, w_out, w_out, w_out, w_out)


def _mlstm_step(qb, kb, kt, va, b, u_row, mask, g, st_ref, m_st):
    dmat = jnp.where(mask, b + u_row, NEG)
    inter = b + m_st
    m_t = jnp.maximum(inter, jnp.max(dmat, axis=-1, keepdims=True))
    s = _dot_nt(qb, kb) * jnp.exp(dmat - m_t)
    st = st_ref[...]
    r_intra = _dot(s.astype(BF16), va)
    r_inter = _dot(qb, st.astype(BF16))
    w_inter = jnp.exp(inter - m_t)
    num = r_intra[:, :HEAD_DIM] + w_inter * r_inter[:, :HEAD_DIM]
    den = r_intra[:, HEAD_DIM:] + w_inter * r_inter[:, HEAD_DIM:]
    h_out = num / jnp.maximum(jnp.abs(den), jnp.exp(-m_t))
    a = g + u_row
    m_new = jnp.maximum(g + m_st, jnp.max(a, axis=-1, keepdims=True))
    ktw = (kt * jnp.exp(a - m_new)).astype(BF16)
    st_ref[...] = jnp.exp(g + m_st - m_new) * st + _dot(ktw, va)
    return h_out, m_new


_MLSTM_HEADS = 2


def _mlstm_kernel(q_ref, k_ref, v_ref, o_ref, gt_ref, cwq_ref, cwk_ref, cbq_ref, cbk_ref, gb_ref, gain_ref,
                  out_ref, qs_ref, ks_ref, kt_ref, va_ref, bp_ref, bs_ref, ut_ref, bf_ref, bb_ref, hf_ref, hb_ref,
                  sf_ref, sb_ref):
    t_len = q_ref.shape[1]
    n_chunks = t_len // CHUNK
    row = lax.broadcasted_iota(jnp.int32, (t_len, 1), 0)
    lane = lax.broadcasted_iota(jnp.int32, (1, LANES), 1)

    @pl.when(pl.program_id(1) == 0)
    def _():
        gates = gt_ref[...] + gb_ref[...]
        logf = _log_sigmoid(gates)
        rin = row & (CHUNK - 1)
        bp = logf
        bs = logf
        k = 1
        while k < CHUNK:
            bp = bp + jnp.where(rin >= k, pltpu.roll(bp, k, 0), 0.0)
            bs = bs + jnp.where(rin < CHUNK - k, pltpu.roll(bs, t_len - k, 0), 0.0)
            k *= 2
        bp_ref[...] = bp
        bs_ref[...] = bs
        u = gates - jnp.where(lane < 2 * H_MLSTM, pltpu.roll(bp, LANES - H_MLSTM, 1),
                              pltpu.roll(bs, LANES - H_MLSTM, 1))
        for c in range(n_chunks):
            ut_ref[:, c * CHUNK:(c + 1) * CHUNK] = u[c * CHUNK:(c + 1) * CHUNK, :].T[0:ut_ref.shape[0], :]

    def conv_silu(x, w, b):
        prev = jnp.where(row >= 1, pltpu.roll(x, 1, 0), 0.0)
        nxt = jnp.where(row <= t_len - 2, pltpu.roll(x, t_len - 1, 0), 0.0)
        y = prev * w[0:1, :] + x * w[1:2, :] + nxt * w[2:3, :] + b
        return y * _sigmoid(y)

    def lane_bcast(x, idx):
        return jnp.broadcast_to(jnp.sum(jnp.where(lane == idx, x, 0.0), axis=-1, keepdims=True), x.shape)

    ti = lax.broadcasted_iota(jnp.int32, (CHUNK, CHUNK), 0)
    si = lax.broadcasted_iota(jnp.int32, (CHUNK, CHUNK), 1)
    mask_f = ti >= si
    mask_b = ti <= si
    sub = lax.broadcasted_iota(jnp.int32, (2 * H_MLSTM, 1), 0)

    for hh in range(_MLSTM_HEADS):
        head = pl.program_id(1) * _MLSTM_HEADS + hh
        hs = slice(hh * HEAD_DIM, (hh + 1) * HEAD_DIM)

        def rowsel(x, head=head):
            return jnp.sum(jnp.where(sub == head, x, 0.0), axis=0, keepdims=True)

        qs_ref[hh] = (conv_silu(q_ref[hh], cwq_ref[:, hs], cbq_ref[:, hs]) * (HEAD_DIM ** -0.5)).astype(BF16)
        k = conv_silu(k_ref[hh], cwk_ref[:, hs], cbk_ref[:, hs])
        ks_ref[hh] = k.astype(BF16)
        for c in range(n_chunks):
            kt_ref[hh, :, c * CHUNK:(c + 1) * CHUNK] = k[c * CHUNK:(c + 1) * CHUNK, :].T
        va_ref[hh, :, :HEAD_DIM] = v_ref[hh].astype(BF16)
        va_ref[hh, :, HEAD_DIM:] = jnp.ones((t_len, HEAD_DIM), BF16)
        bf_ref[hh] = lane_bcast(bp_ref[...], H_MLSTM + head)
        bb_ref[hh] = lane_bcast(bs_ref[...], 3 * H_MLSTM + head)
        sf_ref[hh] = jnp.zeros(sf_ref.shape[1:], F32)
        sb_ref[hh] = jnp.zeros(sb_ref.shape[1:], F32)

        mf = mb = jnp.full((1, 1), NEG, F32)
        for i in range(n_chunks):
            sf = slice(i * CHUNK, (i + 1) * CHUNK)
            sb = slice((n_chunks - 1 - i) * CHUNK, (n_chunks - i) * CHUNK)

            b = bf_ref[hh, sf, :]
            hf, mf = _mlstm_step(qs_ref[hh, sf, :], ks_ref[hh, sf, :], kt_ref[hh, :, sf], va_ref[hh, sf, :], b,
                                 rowsel(ut_ref[0:2 * H_MLSTM, sf]), mask_f, b[CHUNK - 1:CHUNK, 0:1],
                                 sf_ref.at[hh], mf)
            hf_ref[hh, sf, :] = hf

            b = bb_ref[hh, sb, :]
            hb, mb = _mlstm_step(qs_ref[hh, sb, :], ks_ref[hh, sb, :], kt_ref[hh, :, sb], va_ref[hh, sb, :], b,
                                 rowsel(ut_ref[2 * H_MLSTM:4 * H_MLSTM, sb]), mask_b, b[0:1, 0:1],
                                 sb_ref.at[hh], mb)
            hb_ref[hh, sb, :] = hb

        y = _sigmoid(o_ref[hh]) * (hf_ref[hh] + hb_ref[hh])
        out_ref[:, hs] = _rms(y, gain_ref[:, hs]).astype(out_ref.dtype)


def _mlstm(proj, gates, conv_w, conv_b, gate_b, gain, *, batch, t_len):
    nh = _MLSTM_HEADS
    w = H_MLSTM * HEAD_DIM
    assert H_MLSTM % nh == 0 and all(b % nh == 0 for b in (_BLK_MQ, _BLK_MK, _BLK_MV, _BLK_MO))
    blk = lambda off: pl.BlockSpec((nh, t_len, HEAD_DIM), lambda b, h: (off // nh + h, b, 0))
    vec = lambda off: pl.BlockSpec((1, nh * HEAD_DIM), lambda b, h: (0, off // nh + h))
    seq = pltpu.VMEM((nh, t_len, HEAD_DIM), F32)
    seqb = pltpu.VMEM((nh, t_len, HEAD_DIM), BF16)
    shared = pltpu.VMEM((t_len, HEAD_DIM), F32)
    state = pltpu.VMEM((nh, HEAD_DIM, 2 * HEAD_DIM), F32)
    return pl.pallas_call(
        _mlstm_kernel,
        grid=(batch, H_MLSTM // nh),
        in_specs=[
            blk(_BLK_MQ), blk(_BLK_MK), blk(_BLK_MV), blk(_BLK_MO),
            pl.BlockSpec((t_len, LANES), lambda b, h: (b, 0)),
            pl.BlockSpec((3, nh * HEAD_DIM), lambda b, h: (0, h)),
            pl.BlockSpec((3, nh * HEAD_DIM), lambda b, h: (0, H_MLSTM // nh + h)),
            vec(0), vec(H_MLSTM),
            pl.BlockSpec((1, LANES), lambda b, h: (0, 0)),
            vec(0),
        ],
        out_specs=pl.BlockSpec((t_len, nh * HEAD_DIM), lambda b, h: (b, h)),
        out_shape=jax.ShapeDtypeStruct((batch * t_len, w), BF16),
        scratch_shapes=[
            seqb, seqb, pltpu.VMEM((nh, HEAD_DIM, t_len), F32), pltpu.VMEM((nh, t_len, 2 * HEAD_DIM), BF16),
            shared, shared, pltpu.VMEM((4 * H_MLSTM, t_len), F32), seq, seq, seq, seq, state, state,
        ],
        compiler_params=_params("parallel", "arbitrary"),
        name="mlstm",
    )(proj, proj, proj, proj, gates, conv_w, conv_w, conv_b.reshape(1, -1), conv_b.reshape(1, -1),
      gate_b, gain.reshape(1, -1))


_NA_HEADS = 4


def _na_kernel(q_ref, k_ref, v_ref, bias_ref, out_ref, qb_ref, kb_ref, va_ref):
    t_len = q_ref.shape[1]
    rows = t_len // GRID_W
    wr = min(WIN_R, rows)
    n_keys = wr * GRID_W

    for hh in range(_NA_HEADS):
        qb_ref[hh] = (q_ref[hh] * (HEAD_DIM ** -0.5)).astype(BF16)
        kb_ref[hh] = k_ref[hh].astype(BF16)
        va_ref[hh, :, :HEAD_DIM] = v_ref[hh].astype(BF16)
        va_ref[hh, :, HEAD_DIM:] = jnp.ones((t_len, HEAD_DIM), BF16)

    for r in range(rows):
        r0 = min(max(r - wr // 2, 0), rows - wr)
        sq = slice(r * GRID_W, (r + 1) * GRID_W)
        sk = slice(r0 * GRID_W, r0 * GRID_W + n_keys)
        pairs = [WIN_R - 1 + r0 + 2 * c - r for c in range(wr // 2)]
        for hh in range(_NA_HEADS):
            bias = jnp.concatenate([bias_ref[hh, a] for a in pairs], axis=1)
            s = _dot_nt(qb_ref[hh, sq, :], kb_ref[hh, sk, :]) + bias
            e = jnp.exp(s - jnp.max(s, axis=-1, keepdims=True))
            o = _dot(e.astype(BF16), va_ref[hh, sk, :])
            out_ref[sq, hh * HEAD_DIM:(hh + 1) * HEAD_DIM] = (o[:, :HEAD_DIM] / o[:, HEAD_DIM:]).astype(out_ref.dtype)


def _na_bias_pairs(rpb):
    qc = np.arange(GRID_W)[:, None]
    kc = np.arange(GRID_W)[None, :]
    win_c0 = np.clip(qc - WIN_C // 2, 0, GRID_W - WIN_C)
    ok = (kc >= win_c0) & (kc < win_c0 + WIN_C)
    sel_c = np.arange(2 * WIN_C - 1) == (kc - qc + WIN_C - 1)[..., None]
    t = jnp.einsum("hrc,qjc->hrqj", rpb.astype(F32), jnp.asarray(sel_c, F32), precision=lax.Precision.HIGHEST)
    t = jnp.where(jnp.asarray(ok), t, NEG)
    return jnp.concatenate([t[:, :-1], t[:, 1:]], axis=-1)


def _na(proj, bias_tab, *, batch, t_len):
    assert min(WIN_R, t_len // GRID_W) % 2 == 0 and 2 * GRID_W == LANES
    assert H_NA % _NA_HEADS == 0 and _BLK_NQ % _NA_HEADS == 0 and _BLK_NK % _NA_HEADS == 0 and _BLK_NV % _NA_HEADS == 0
    blk = lambda off: pl.BlockSpec((_NA_HEADS, t_len, HEAD_DIM), lambda b, h: (off // _NA_HEADS + h, b, 0))
    seq = pltpu.VMEM((_NA_HEADS, t_len, HEAD_DIM), BF16)
    return pl.pallas_call(
        _na_kernel,
        grid=(batch, H_NA // _NA_HEADS),
        in_specs=[
            blk(_BLK_NQ), blk(_BLK_NK), blk(_BLK_NV),
            pl.BlockSpec((_NA_HEADS,) + bias_tab.shape[1:], lambda b, h: (h, 0, 0, 0)),
        ],
        out_specs=pl.BlockSpec((t_len, _NA_HEADS * HEAD_DIM), lambda b, h: (b, h)),
        out_shape=jax.ShapeDtypeStruct((batch * t_len, H_NA * HEAD_DIM), BF16),
        scratch_shapes=[seq, seq, pltpu.VMEM((_NA_HEADS, t_len, 2 * HEAD_DIM), BF16)],
        compiler_params=_params("parallel", "parallel"),
        name="natten",
    )(proj, proj, proj, bias_tab)


_RET_HEADS = 2


def _ret_kernel(q_ref, k_ref, v_ref, g_ref, cos_ref, sin_ref, dl_ref, gain_ref, out_ref,
                qb_ref, qx_ref, kb_ref, kz_ref, vb_ref, o_ref, kv_ref, r_ref):
    t_len = q_ref.shape[1]
    n_chunks = t_len // CHUNK

    def rope(x):
        return x * cos_ref[...] + pltpu.roll(x, HEAD_DIM // 2, 1) * sin_ref[...]

    def chunk(c):
        return slice(c * CHUNK, (c + 1) * CHUNK)

    diff = (lax.broadcasted_iota(jnp.int32, (CHUNK, CHUNK), 0)
            - lax.broadcasted_iota(jnp.int32, (CHUNK, CHUNK), 1)).astype(F32)
    pos = lax.broadcasted_iota(jnp.int32, (CHUNK, HEAD_DIM), 0).astype(F32)

    for hh in range(_RET_HEADS):
        lg = _log_sigmoid(dl_ref[hh])
        lg_f, lg_b = lg[0:1, :], lg[1:2, :]
        intra = (jnp.where(diff >= 0, jnp.exp(jnp.where(diff >= 0, diff, 0.0) * lg_f), 0.0)
                 + jnp.where(diff <= 0, jnp.exp(jnp.where(diff <= 0, -diff, 0.0) * lg_b), 0.0))
        gch_f = jnp.exp(CHUNK * lg_f)
        gch_b = jnp.exp(CHUNK * lg_b)
        xi_f = jnp.exp((pos + 1.0) * lg_f)
        xi_b = jnp.exp((CHUNK - pos) * lg_b)
        zeta_f = jnp.exp((CHUNK - 1.0 - pos) * lg_f)
        zeta_b = jnp.exp(pos * lg_b)
        q = rope(q_ref[hh])
        k = rope(k_ref[hh]) * (HEAD_DIM ** -0.5)
        qb_ref[hh] = q.astype(BF16)
        kb_ref[hh] = k.astype(BF16)
        vb_ref[hh] = v_ref[hh].astype(BF16)
        for c in range(n_chunks):
            cs = chunk(c)
            qx_ref[hh, cs, :HEAD_DIM] = (q[cs, :] * xi_f).astype(BF16)
            qx_ref[hh, cs, HEAD_DIM:] = (q[cs, :] * xi_b).astype(BF16)
            kz_ref[hh, :HEAD_DIM, cs] = (k[cs, :] * zeta_f).T.astype(BF16)
            kz_ref[hh, HEAD_DIM:, cs] = (k[cs, :] * zeta_b).T.astype(BF16)

        for c in range(n_chunks):
            sl = chunk(c)
            vb = vb_ref[hh, sl, :]
            s = _dot_nt(qb_ref[hh, sl, :], kb_ref[hh, sl, :]) * intra
            o_ref[hh, sl, :] = _dot(s.astype(BF16), vb)
            kv_ref[hh, c] = _dot(kz_ref[hh, :, sl], vb)

        def state_pass(i, carry, hh=hh, gch_f=gch_f, gch_b=gch_b):
            rf, rb = carry
            cb = n_chunks - 1 - i
            r_ref[hh, i, :HEAD_DIM, :] = rf.astype(BF16)
            r_ref[hh, cb, HEAD_DIM:, :] = rb.astype(BF16)
            return gch_f * rf + kv_ref[hh, i, :HEAD_DIM, :], gch_b * rb + kv_ref[hh, cb, HEAD_DIM:, :]

        zero = jnp.zeros((HEAD_DIM, HEAD_DIM), F32)
        lax.fori_loop(0, n_chunks, state_pass, (zero, zero))

        for c in range(n_chunks):
            o_ref[hh, chunk(c), :] += _dot(qx_ref[hh, chunk(c), :], r_ref[hh, c])

        g = g_ref[hh]
        hs = slice(hh * HEAD_DIM, (hh + 1) * HEAD_DIM)
        out_ref[:, hs] = (g * _sigmoid(g) * _rms(o_ref[hh], gain_ref[:, hs])).astype(out_ref.dtype)


def _ret(proj, cos2, sin2, decay_rows, gain, *, batch, t_len):
    n_chunks = t_len // CHUNK
    nh = _RET_HEADS
    assert H_RET % nh == 0 and all(b % nh == 0 for b in (_BLK_RQ, _BLK_RK, _BLK_RV, _BLK_RG))
    blk = lambda off: pl.BlockSpec((nh, t_len, HEAD_DIM), lambda b, h: (off // nh + h, b, 0))
    tab = pl.BlockSpec((t_len, HEAD_DIM), lambda b, h: (0, 0))
    seqb = pltpu.VMEM((nh, t_len, HEAD_DIM), BF16)
    return pl.pallas_call(
        _ret_kernel,
        grid=(batch, H_RET // nh),
        in_specs=[
            blk(_BLK_RQ), blk(_BLK_RK), blk(_BLK_RV), blk(_BLK_RG), tab, tab,
            pl.BlockSpec((nh, 2, LANES), lambda b, h: (h, 0, 0)),
            pl.BlockSpec((1, nh * HEAD_DIM), lambda b, h: (0, h)),
        ],
        out_specs=pl.BlockSpec((t_len, nh * HEAD_DIM), lambda b, h: (b, h)),
        out_shape=jax.ShapeDtypeStruct((batch * t_len, H_RET * HEAD_DIM), BF16),
        scratch_shapes=[
            seqb, pltpu.VMEM((nh, t_len, 2 * HEAD_DIM), BF16), seqb, pltpu.VMEM((nh, 2 * HEAD_DIM, t_len), BF16), seqb,
            pltpu.VMEM((nh, t_len, HEAD_DIM), F32),
            pltpu.VMEM((nh, n_chunks, 2 * HEAD_DIM, HEAD_DIM), F32),
            pltpu.VMEM((nh, n_chunks, 2 * HEAD_DIM, HEAD_DIM), BF16),
        ],
        compiler_params=_params("parallel", "parallel"),
        name="retention",
    )(proj, proj, proj, proj, cos2, sin2, decay_rows, gain.reshape(1, -1))


def _rope_tables(t_len):
    inv = ROPE_BASE ** (-jnp.arange(0, HEAD_DIM, 2, dtype=F32) / HEAD_DIM)
    ang = jnp.arange(t_len, dtype=F32)[:, None] * inv[None, :]
    cos, sin = jnp.cos(ang), jnp.sin(ang)
    return jnp.concatenate([cos, cos], axis=-1), jnp.concatenate([-sin, sin], axis=-1)


def _pick(n, pref):
    return pref if n % pref == 0 else n


def kernel(x, ffn1_norm, ffn1_w_gu, ffn1_w_down, mix_norm, w_in, mlstm_conv_w, mlstm_conv_b, mlstm_gate_b,
           mlstm_head_norm, na_rpb, ret_decay_logit, ret_head_norm, w_out, ffn2_norm, ffn2_w_gu, ffn2_w_down,
           final_norm):
    batch, t_len, d = x.shape
    n = batch * t_len
    depth = ffn1_norm.shape[0]
    n_gates = 4 * H_MLSTM
    gate_lo = 4 * H_MLSTM * HEAD_DIM

    tm = _pick(n, 1024)
    ffn = functools.partial(_ffn, tm=tm, tf=512)
    cos2, sin2 = _rope_tables(t_len)

    h = x.reshape(n, d)
    w_gu1, w_down1 = _cast_layer(ffn1_w_gu, 0, tr=256), _cast_layer(ffn1_w_down, 0, tr=704)
    for l in range(depth):
        h, (w_gu2, w_down2) = ffn(h, ffn1_norm[l], w_gu1, w_down1, final_norm, final_norm=False,
                                  cast=((ffn2_w_gu, l), (ffn2_w_down, l)))

        w_main, w_gate = _cast_w_in(w_in, l, gate_lo=gate_lo, n_gates=n_gates, tc=512)
        proj, gates, wo = _mix_in(h, mix_norm[l], w_main, w_gate, w_out, l, tm=tm, tn=1792)

        gate_b = jnp.pad(mlstm_gate_b[l].astype(F32).reshape(1, n_gates), ((0, 0), (0, LANES - n_gates)))
        m_out = _mlstm(proj, gates, mlstm_conv_w[l], mlstm_conv_b[l], gate_b, mlstm_head_norm[l],
                       batch=batch, t_len=t_len)
        n_out = _na(proj, _na_bias_pairs(na_rpb[l]), batch=batch, t_len=t_len)
        decay_rows = jnp.broadcast_to(ret_decay_logit[l].astype(F32).T[:, :, None], (H_RET, 2, LANES))
        r_out = _ret(proj, cos2, sin2, decay_rows, ret_head_norm[l], batch=batch, t_len=t_len)

        h = _mix_out(h, m_out, n_out, r_out, wo, tm=_pick(n, 512), tn=d)

        last = l == depth - 1
        h, nxt = ffn(h, ffn2_norm[l], w_gu2, w_down2, final_norm, final_norm=last,
                     cast=() if last else ((ffn1_w_gu, l + 1), (ffn1_w_down, l + 1)))
        if not last:
            w_gu1, w_down1 = nxt
    return h.reshape(batch, t_len, d)
```
